```python
import jax, jax.numpy as jnp
from jax import lax
import numpy as np

D_MODEL = 2048
BATCH = 8
SEQ = 2048
DEPTH = 1

CHUNK = 64
N_META = 16
D_CONF = D_MODEL // 2
D_SHORT = D_MODEL // 2
CONF_KERNEL = 31
SHORT_KERNEL = 3
D_FF = 4 * D_MODEL
IN_COLS = 2 * D_CONF + 3 * D_SHORT + 2 * D_MODEL
RMS_EPS = 1e-6
LN_EPS = 1e-5

kernel_name = "hybrid_gated_conformer_shortconv_block"


def rms_norm(x, g):
    xf = x.astype(jnp.float32)
    y = xf * lax.rsqrt(jnp.mean(xf * xf, axis=-1, keepdims=True) + RMS_EPS)
    return (y * g.astype(jnp.float32)).astype(x.dtype)


def layer_norm(x, g, b):
    xf = x.astype(jnp.float32)
    mu = jnp.mean(xf, axis=-1, keepdims=True)
    var = jnp.mean(jnp.square(xf - mu), axis=-1, keepdims=True)
    y = (xf - mu) * lax.rsqrt(var + LN_EPS)
    return (y * g.astype(jnp.float32) + b.astype(jnp.float32)).astype(x.dtype)


def causal_depthwise_conv(x, w, b=None):
    k = w.shape[0]
    y = lax.conv_general_dilated(
        x, w[:, None, :].astype(x.dtype),
        window_strides=(1,),
        padding=[(k - 1, 0)],
        dimension_numbers=("NWC", "WIO", "NWC"),
        feature_group_count=x.shape[-1])
    if b is not None:
        y = y + b.astype(x.dtype)
    return y


def setup_inputs(seed: int = 0) -> dict:
    key = jax.random.key(seed)
    ks = jax.random.split(key, 24)
    f32 = jnp.float32

    def nrm(k, shape, scale):
        return jax.random.normal(k, shape, f32) * scale

    def gain(k, shape):
        return 1.0 + 0.05 * jax.random.normal(k, shape, f32)

    return {
        "x": jax.random.normal(ks[0], (BATCH, SEQ, D_MODEL), f32),
        "meta": nrm(ks[1], (N_META, D_MODEL), 1.0),
        "g_pre_mix": gain(ks[2], (DEPTH, D_MODEL)),
        "w_in": nrm(ks[3], (DEPTH, D_MODEL, IN_COLS), D_MODEL ** -0.5),
        "b_gates": nrm(ks[4], (DEPTH, 2 * D_MODEL), 0.1),
        "conf_dw_w": nrm(ks[5], (DEPTH, CONF_KERNEL, D_CONF), CONF_KERNEL ** -0.5),
        "conf_dw_b": nrm(ks[6], (DEPTH, D_CONF), 0.02),
        "conf_ln_g": gain(ks[7], (DEPTH, D_CONF)),
        "conf_ln_b": nrm(ks[8], (DEPTH, D_CONF), 0.02),
        "conf_w_pw": nrm(ks[9], (DEPTH, D_CONF, D_MODEL), D_CONF ** -0.5),
        "short_dw_w": nrm(ks[10], (DEPTH, SHORT_KERNEL, D_SHORT), SHORT_KERNEL ** -0.5),
        "short_w_out": nrm(ks[11], (DEPTH, D_SHORT, D_MODEL), D_SHORT ** -0.5),
        "w_o": nrm(ks[12], (DEPTH, D_MODEL, D_MODEL), D_MODEL ** -0.5),
        "g_post_mix": gain(ks[13], (DEPTH, D_MODEL)),
        "g_pre_mlp": gain(ks[14], (DEPTH, D_MODEL)),
        "w_up": nrm(ks[15], (DEPTH, D_MODEL, D_FF), D_MODEL ** -0.5),
        "w_down": nrm(ks[16], (DEPTH, D_FF, D_MODEL), D_FF ** -0.5),
        "g_post_mlp": gain(ks[17], (DEPTH, D_MODEL)),
    }


def reference(x, meta, g_pre_mix, w_in, b_gates, conf_dw_w, conf_dw_b, conf_ln_g,
              conf_ln_b, conf_w_pw, short_dw_w, short_w_out, w_o, g_post_mix,
              g_pre_mlp, w_up, w_down, g_post_mlp):
    bsz = x.shape[0]
    meta_b = jnp.broadcast_to(meta.astype(x.dtype)[None], (bsz, N_META, D_MODEL))
    h = jnp.concatenate([meta_b, x], axis=1)

    for l in range(DEPTH):
        n = rms_norm(h, g_pre_mix[l])
        proj = jnp.einsum("btd,dc->btc", n, w_in[l])
        o1 = 2 * D_CONF
        o2 = o1 + 3 * D_SHORT
        u_a = proj[..., :o1]
        u_b = proj[..., o1:o2]
        gates = jax.nn.sigmoid(proj[..., o2:] + b_gates[l])
        gate_a = gates[..., :D_MODEL]
        gate_b = gates[..., D_MODEL:]

        a_val, a_gate = jnp.split(u_a, 2, axis=-1)
        a = a_val * jax.nn.sigmoid(a_gate)
        a = causal_depthwise_conv(a, conf_dw_w[l], conf_dw_b[l])
        a = jax.nn.silu(layer_norm(a, conf_ln_g[l], conf_ln_b[l]))
        y_a = jnp.einsum("btc,cd->btd", a, conf_w_pw[l])

        b_g, c_g, v = jnp.split(u_b, 3, axis=-1)
        s = b_g * causal_depthwise_conv(c_g * v, short_dw_w[l])
        y_b = jnp.einsum("btc,cd->btd", s, short_w_out[l])

        m = gate_a * y_a + gate_b * y_b
        mix = jnp.einsum("btd,de->bte", m, w_o[l])
        h = h + rms_norm(mix, g_post_mix[l])

        n2 = rms_norm(h, g_pre_mlp[l])
        f = jnp.square(jax.nn.relu(jnp.einsum("btd,df->btf", n2, w_up[l])))
        f = jnp.einsum("btf,fd->btd", f, w_down[l])
        h = h + rms_norm(f, g_post_mlp[l])

    return h[:, N_META:, :]
```

```python
import functools

import jax
import jax.numpy as jnp
from jax import lax
from jax.experimental import pallas as pl
from jax.experimental.pallas import tpu as pltpu

RMS_EPS = 1e-6
LN_EPS = 1e-5

LANES = 128
SUBLANES_F32 = 8
SUBLANES_BF16 = 16
VMEM_LIMIT_BYTES = 58 * 1024 * 1024

BF16 = jnp.bfloat16
F32 = jnp.float32


def _sigmoid(x):
    return 1.0 / (1.0 + jnp.exp(-x))


def _rms_in_proj_kernel(x_ref, g_ref, w_ref, o_ref, n_ref):
    @pl.when(pl.program_id(1) == 0)
    def _normalise():
        x = x_ref[...]
        inv = lax.rsqrt(jnp.mean(x * x, axis=-1, keepdims=True) + RMS_EPS)
        n_ref[...] = (x * inv * g_ref[...]).astype(BF16)

    o_ref[...] = jnp.dot(n_ref[...], w_ref[...],
                         preferred_element_type=F32).astype(o_ref.dtype)


def _rms_in_proj(x2d, gain, w_bf16, *, tm, tn):
    m, d = x2d.shape
    n = w_bf16.shape[1]
    return pl.pallas_call(
        _rms_in_proj_kernel,
        grid=(m // tm, n // tn),
        in_specs=[
            pl.BlockSpec((tm, d), lambda i, j: (i, 0)),
            pl.BlockSpec((1, d), lambda i, j: (0, 0)),
            pl.BlockSpec((d, tn), lambda i, j: (0, j)),
        ],
        out_specs=pl.BlockSpec((tm, tn), lambda i, j: (i, j)),
        out_shape=jax.ShapeDtypeStruct((m, n), BF16),
        scratch_shapes=[pltpu.VMEM((tm, d), BF16)],
        compiler_params=pltpu.CompilerParams(
            dimension_semantics=("parallel", "arbitrary"),
            vmem_limit_bytes=VMEM_LIMIT_BYTES),
        name="rms_in_proj",
    )(x2d, gain, w_bf16)


def _mixer_kernel(p_ref, meta_ref, bg_ref, cw_ref, cb_ref, lg_ref, lb_ref,
                  pw_ref, sw_ref, wo_ref, m_ref,
                  a_ext, cv_ext, conv_buf, act_a, act_s, *,
                  tt, dc, ds, dm, k_conf, k_short, n_meta, hist_a, hist_s,
                  rows, n_chunk):
    o_ag = dc
    o_bg = 2 * dc
    o_cg = o_bg + ds
    o_v = o_cg + ds
    o_ga = o_v + ds
    o_gb = o_ga + dm
    n_ca = dc // LANES
    n_cs = ds // LANES

    def glu(val, gate):
        return val.astype(F32) * _sigmoid(gate.astype(F32))

    @pl.when(pl.program_id(1) == 0)
    def _seed_history():
        mp = meta_ref[...]
        a_meta = glu(mp[:, 0:dc], mp[:, o_ag:o_ag + dc])
        cv_meta = mp[:, o_cg:o_cg + ds].astype(F32) * mp[:, o_v:o_v + ds].astype(F32)
        for c in range(n_ca):
            a_ext[c, 0:hist_a - n_meta, :] = jnp.zeros((hist_a - n_meta, LANES), F32)
            a_ext[c, hist_a - n_meta:hist_a, :] = a_meta[:, c * LANES:(c + 1) * LANES]
        for c in range(n_cs):
            cv_ext[c, 0:hist_s, :] = cv_meta[n_meta - hist_s:, c * LANES:(c + 1) * LANES]

    n_row_chunks = tt // rows

    def fill(r, carry):
        r0 = pl.multiple_of(r * rows, rows)
        a = glu(p_ref[pl.ds(r0, rows), 0:dc], p_ref[pl.ds(r0, rows), o_ag:o_ag + dc])
        cv = (p_ref[pl.ds(r0, rows), o_cg:o_cg + ds].astype(F32)
              * p_ref[pl.ds(r0, rows), o_v:o_v + ds].astype(F32))
        for c in range(n_ca):
            a_ext[c, pl.ds(hist_a + r0, rows), :] = a[:, c * LANES:(c + 1) * LANES]
        for c in range(n_cs):
            cv_ext[c, pl.ds(hist_s + r0, rows), :] = cv[:, c * LANES:(c + 1) * LANES]
        return carry

    lax.fori_loop(0, n_row_chunks, fill, 0)

    def conv_a(r, carry):
        r0 = pl.multiple_of(r * rows, rows)
        for c in range(n_ca):
            lanes = slice(c * LANES, (c + 1) * LANES)
            acc = jnp.broadcast_to(cb_ref[:, lanes], (rows, LANES))
            for k in range(k_conf):
                start = r0 + (hist_a - (k_conf - 1) + k)
                acc = acc + cw_ref[k:k + 1, lanes] * a_ext[c, pl.ds(start, rows), :]
            conv_buf[c, pl.ds(r0, rows), :] = acc
        return carry

    lax.fori_loop(0, n_row_chunks, conv_a, 0)

    def norm_a(r, carry):
        r0 = pl.multiple_of(r * rows, rows)
        xs = [conv_buf[c, pl.ds(r0, rows), :] for c in range(n_ca)]
        tot = xs[0]
        for x in xs[1:]:
            tot = tot + x
        mu = jnp.sum(tot, axis=-1, keepdims=True) * (1.0 / dc)
        ds_ = [x - mu for x in xs]
        sq = ds_[0] * ds_[0]
        for d in ds_[1:]:
            sq = sq + d * d
        var = jnp.sum(sq, axis=-1, keepdims=True) * (1.0 / dc)
        inv = lax.rsqrt(var + LN_EPS)
        for c in range(n_ca):
            lanes = slice(c * LANES, (c + 1) * LANES)
            y = ds_[c] * inv * lg_ref[:, lanes] + lb_ref[:, lanes]
            act_a[pl.ds(r0, rows), lanes] = (y * _sigmoid(y)).astype(BF16)
        return carry

    lax.fori_loop(0, n_row_chunks, norm_a, 0)

    def conv_s(r, carry):
        r0 = pl.multiple_of(r * rows, rows)
        for c in range(n_cs):
            lanes = slice(c * LANES, (c + 1) * LANES)
            acc = None
            for k in range(k_short):
                start = r0 + (hist_s - (k_short - 1) + k)
                term = sw_ref[k:k + 1, lanes] * cv_ext[c, pl.ds(start, rows), :]
                acc = term if acc is None else acc + term
            bgate = p_ref[pl.ds(r0, rows), o_bg + c * LANES:o_bg + (c + 1) * LANES]
            act_s[pl.ds(r0, rows), lanes] = (bgate.astype(F32) * acc).astype(BF16)
        return carry

    lax.fori_loop(0, n_row_chunks, conv_s, 0)

    for c in range(n_ca):
        a_ext[c, 0:hist_a, :] = a_ext[c, tt:tt + hist_a, :]
    for c in range(n_cs):
        cv_ext[c, 0:hist_s, :] = cv_ext[c, tt:tt + hist_s, :]

    xa = act_a[...]
    xs_ = act_s[...]
    for n0 in range(0, dm, n_chunk):
        cols = slice(n0, n0 + n_chunk)
        ya = jnp.dot(xa, pw_ref[:, cols], preferred_element_type=F32)
        yb = jnp.dot(xs_, wo_ref[:, cols], preferred_element_type=F32)
        ga = _sigmoid(p_ref[:, o_ga + n0:o_ga + n0 + n_chunk].astype(F32)
                      + bg_ref[:, cols])
        gb = _sigmoid(p_ref[:, o_gb + n0:o_gb + n0 + n_chunk].astype(F32)
                      + bg_ref[:, dm + n0:dm + n0 + n_chunk])
        m_ref[:, cols] = (ga * ya + gb * yb).astype(m_ref.dtype)


def _mixer(proj, meta_proj, b_gates, conf_dw_w, conf_dw_b, conf_ln_g, conf_ln_b,
           pw_bf16, short_dw_w, wout_bf16, *, batch, seq, dm, tt):
    dc = conf_dw_w.shape[1]
    ds = short_dw_w.shape[1]
    k_conf = conf_dw_w.shape[0]
    k_short = short_dw_w.shape[0]
    n_meta = meta_proj.shape[0]
    n_cols = proj.shape[1]
    hist_a = 2 * SUBLANES_BF16
    hist_s = SUBLANES_F32
    assert hist_a >= k_conf - 1 and hist_a >= n_meta and n_meta >= hist_s >= k_short - 1
    n_t = seq // tt
    kern = functools.partial(
        _mixer_kernel, tt=tt, dc=dc, ds=ds, dm=dm, k_conf=k_conf, k_short=k_short,
        n_meta=n_meta, hist_a=hist_a, hist_s=hist_s, rows=2 * SUBLANES_BF16,
        n_chunk=4 * LANES)
    const = lambda shape: pl.BlockSpec(shape, lambda b, t: (0, 0))
    return pl.pallas_call(
        kern,
        grid=(batch, n_t),
        in_specs=[
            pl.BlockSpec((tt, n_cols), lambda b, t: (b * n_t + t, 0)),
            const((n_meta, n_cols)),
            const((1, 2 * dm)),
            const((k_conf, dc)),
            const((1, dc)),
            const((1, dc)),
            const((1, dc)),
            const((dc, dm)),
            const((k_short, ds)),
            const((ds, dm)),
        ],
        out_specs=pl.BlockSpec((tt, dm), lambda b, t: (b * n_t + t, 0)),
        out_shape=jax.ShapeDtypeStruct((batch * seq, dm), BF16),
        scratch_shapes=[
            pltpu.VMEM((dc // LANES, hist_a + tt, LANES), F32),
            pltpu.VMEM((ds // LANES, hist_s + tt, LANES), F32),
            pltpu.VMEM((dc // LANES, tt, LANES), F32),
            pltpu.VMEM((tt, dc), BF16),
            pltpu.VMEM((tt, ds), BF16),
        ],
        compiler_params=pltpu.CompilerParams(
            dimension_semantics=("parallel", "arbitrary"),
            vmem_limit_bytes=VMEM_LIMIT_BYTES),
        name="mixer",
    )(proj, meta_proj, b_gates, conf_dw_w, conf_dw_b, conf_ln_g, conf_ln_b,
      pw_bf16, short_dw_w, wout_bf16)


def _mix_out_kernel(m_ref, w_ref, x_ref, g_ref, o_ref):
    mix = jnp.dot(m_ref[...], w_ref[...], preferred_element_type=F32)
    inv = lax.rsqrt(jnp.mean(mix * mix, axis=-1, keepdims=True) + RMS_EPS)
    o_ref[...] = x_ref[...] + mix * inv * g_ref[...]


def _mix_out(m, wo_bf16, x2d, gain, *, tm):
    rows, d = x2d.shape
    return pl.pallas_call(
        _mix_out_kernel,
        grid=(rows // tm,),
        in_specs=[
            pl.BlockSpec((tm, d), lambda i: (i, 0)),
            pl.BlockSpec((d, d), lambda i: (0, 0)),
            pl.BlockSpec((tm, d), lambda i: (i, 0)),
            pl.BlockSpec((1, d), lambda i: (0, 0)),
        ],
        out_specs=pl.BlockSpec((tm, d), lambda i: (i, 0)),
        out_shape=jax.ShapeDtypeStruct((rows, d), F32),
        compiler_params=pltpu.CompilerParams(
            dimension_semantics=("parallel",),
            vmem_limit_bytes=VMEM_LIMIT_BYTES),
        name="mix_out",
    )(m, wo_bf16, x2d, gain)


def _mlp_kernel(h_ref, gpre_ref, wup_ref, wdn_ref, gpost_ref, o_ref, n_ref, *,
                n_chunk):
    j = pl.program_id(1)

    @pl.when(j == 0)
    def _normalise():
        h = h_ref[...]
        inv = lax.rsqrt(jnp.mean(h * h, axis=-1, keepdims=True) + RMS_EPS)
        n_ref[...] = (h * inv * gpre_ref[...]).astype(BF16)

        o_ref[...] = jnp.zeros_like(o_ref)

    u = jnp.dot(n_ref[...], wup_ref[...], preferred_element_type=F32)
    u = jnp.maximum(u, 0.0)
    act = (u * u).astype(BF16)
    for n0 in range(0, o_ref.shape[1], n_chunk):
        cols = slice(n0, n0 + n_chunk)
        o_ref[:, cols] += jnp.dot(act, wdn_ref[:, cols], preferred_element_type=F32)

    @pl.when(j == pl.num_programs(1) - 1)
    def _finish():
        f = o_ref[...]
        inv = lax.rsqrt(jnp.mean(f * f, axis=-1, keepdims=True) + RMS_EPS)
        o_ref[...] = h_ref[...] + f * inv * gpost_ref[...]


def _mlp(h1, g_pre, wup_bf16, wdn_bf16, g_post, *, tm, tf):
    rows, d = h1.shape
    f = wup_bf16.shape[1]
    return pl.pallas_call(
        functools.partial(_mlp_kernel, n_chunk=4 * LANES),
        grid=(rows // tm, f // tf),
        in_specs=[
            pl.BlockSpec((tm, d), lambda i, j: (i, 0)),
            pl.BlockSpec((1, d), lambda i, j: (0, 0)),
            pl.BlockSpec((d, tf), lambda i, j: (0, j)),
            pl.BlockSpec((tf, d), lambda i, j: (j, 0)),
            pl.BlockSpec((1, d), lambda i, j: (0, 0)),
        ],
        out_specs=pl.BlockSpec((tm, d), lambda i, j: (i, 0)),
        out_shape=jax.ShapeDtypeStruct((rows, d), F32),
        scratch_shapes=[pltpu.VMEM((tm, d), BF16)],
        compiler_params=pltpu.CompilerParams(
            dimension_semantics=("parallel", "arbitrary"),
            vmem_limit_bytes=VMEM_LIMIT_BYTES),
        name="mlp",
    )(h1, g_pre, wup_bf16, wdn_bf16, g_post)


def kernel(x, meta, g_pre_mix, w_in, b_gates, conf_dw_w, conf_dw_b, conf_ln_g,
           conf_ln_b, conf_w_pw, short_dw_w, short_w_out, w_o, g_post_mix,
           g_pre_mlp, w_up, w_down, g_post_mlp):
    batch, seq, dm = x.shape
    depth = w_in.shape[0]
    n_meta = meta.shape[0]
    assert depth == 1, "meta-token history seeding assumes a single layer"

    x2d = x.reshape(batch * seq, dm)
    l = 0
    row = lambda v: v[l].reshape(1, -1)
    w_in_b = w_in[l].astype(BF16)
    in_cols = w_in_b.shape[1]

    proj = _rms_in_proj(x2d, row(g_pre_mix), w_in_b, tm=1024, tn=1024)
    meta_proj = _rms_in_proj(meta.astype(x.dtype), row(g_pre_mix), w_in_b,
                             tm=n_meta, tn=in_cols // 3)
    m = _mixer(proj, meta_proj, row(b_gates), conf_dw_w[l], row(conf_dw_b),
               row(conf_ln_g), row(conf_ln_b), conf_w_pw[l].astype(BF16),
               short_dw_w[l], short_w_out[l].astype(BF16),
               batch=batch, seq=seq, dm=dm, tt=256)
    h1 = _mix_out(m, w_o[l].astype(BF16), x2d, row(g_post_mix), tm=512)
    out = _mlp(h1, row(g_pre_mlp), w_up[l].astype(BF16), w_down[l].astype(BF16),
               row(g_post_mlp), tm=1024, tf=512)
    return out.reshape(batch, seq, dm)
```

```python
import functools

import jax
import jax.numpy as jnp
from jax import lax
from jax.experimental import pallas as pl
from jax.experimental.pallas import tpu as pltpu

RMS_EPS = 1e-6
LN_EPS = 1e-5

LANES = 128
SUBLANES_F32 = 8
SUBLANES_BF16 = 16
VMEM_LIMIT_BYTES = 58 * 1024 * 1024

BF16 = jnp.bfloat16
F32 = jnp.float32


def _sigmoid(x):
    return 1.0 / (1.0 + jnp.exp(-x))


def _rms_in_proj_kernel(x_ref, g_ref, w_ref, o_ref, n_ref):
    @pl.when(pl.program_id(1) == 0)
    def _normalise():
        x = x_ref[...]
        inv = lax.rsqrt(jnp.mean(x * x, axis=-1, keepdims=True) + RMS_EPS)
        n_ref[...] = (x * inv * g_ref[...]).astype(BF16)

    o_ref[...] = jnp.dot(n_ref[...], w_ref[...],
                         preferred_element_type=F32).astype(o_ref.dtype)


def _rms_in_proj(x2d, gain, w_bf16, *, tm, tn):
    m, d = x2d.shape
    n = w_bf16.shape[1]
    return pl.pallas_call(
        _rms_in_proj_kernel,
        grid=(m // tm, n // tn),
        in_specs=[
            pl.BlockSpec((tm, d), lambda i, j: (i, 0)),
            pl.BlockSpec((1, d), lambda i, j: (0, 0)),
            pl.BlockSpec((d, tn), lambda i, j: (0, j)),
        ],
        out_specs=pl.BlockSpec((tm, tn), lambda i, j: (i, j)),
        out_shape=jax.ShapeDtypeStruct((m, n), BF16),
        scratch_shapes=[pltpu.VMEM((tm, d), BF16)],
        compiler_params=pltpu.CompilerParams(
            dimension_semantics=("parallel", "arbitrary"),
            vmem_limit_bytes=VMEM_LIMIT_BYTES),
        name="rms_in_proj",
    )(x2d, gain, w_bf16)


def _mixer_kernel(pu_ref, ga0_ref, ga1_ref, gb0_ref, gb1_ref, meta_ref, bg_ref,
                  cw_ref, cb_ref, lg_ref, lb_ref, pw_ref, sw_ref, wo_ref, m_ref,
                  a_ext, cv_ext, act_a, act_s, *,
                  tt, n_t, dc, ds, dm, k_conf, k_short, n_meta, hist_a, hist_s, rows):
    o_ag = dc
    o_bg = 2 * dc
    o_cg = o_bg + ds
    o_v = o_cg + ds
    n_ca = dc // LANES
    n_cs = ds // LANES
    n_row_chunks = tt // rows
    n_chunk = dm // n_row_chunks
    half = dm // 2

    step = pl.program_id(0)
    slot = lax.rem(step, 2)
    prev = 1 - slot

    def glu(val, gate):
        return val.astype(F32) * _sigmoid(gate.astype(F32))

    @pl.when(lax.rem(step, n_t) == 0)
    def _seed_history():
        mp = meta_ref[...]
        a_meta = glu(mp[:, 0:dc], mp[:, o_ag:o_ag + dc])
        cv_meta = mp[:, o_cg:o_cg + ds].astype(F32) * mp[:, o_v:o_v + ds].astype(F32)
        for c in range(n_ca):
            a_ext[c, 0:hist_a - n_meta, :] = jnp.zeros((hist_a - n_meta, LANES), F32)
            a_ext[c, hist_a - n_meta:hist_a, :] = a_meta[:, c * LANES:(c + 1) * LANES]
        for c in range(n_cs):
            cv_ext[c, 0:hist_s, :] = cv_meta[n_meta - hist_s:, c * LANES:(c + 1) * LANES]

    @pl.when(step == 0)
    def _no_previous_block():
        act_a[1] = jnp.zeros(act_a.shape[1:], BF16)
        act_s[1] = jnp.zeros(act_s.shape[1:], BF16)

    def stage1(r0):
        rws = slice(r0, r0 + rows)
        a = glu(pu_ref[rws, 0:dc], pu_ref[rws, o_ag:o_ag + dc])
        cv = pu_ref[rws, o_cg:o_cg + ds].astype(F32) * pu_ref[rws, o_v:o_v + ds].astype(F32)
        for c in range(n_ca):
            a_ext[c, hist_a + r0:hist_a + r0 + rows, :] = a[:, c * LANES:(c + 1) * LANES]
        for c in range(n_cs):
            cv_ext[c, hist_s + r0:hist_s + r0 + rows, :] = cv[:, c * LANES:(c + 1) * LANES]

        xs = []
        for c in range(n_ca):
            lanes = slice(c * LANES, (c + 1) * LANES)
            acc = jnp.broadcast_to(cb_ref[:, lanes], (rows, LANES))
            for k in range(k_conf):
                start = r0 + hist_a - (k_conf - 1) + k
                acc = acc + cw_ref[k:k + 1, lanes] * a_ext[c, start:start + rows, :]
            xs.append(acc)
        tot = xs[0]
        for x in xs[1:]:
            tot = tot + x
        mu = jnp.sum(tot, axis=-1, keepdims=True) * (1.0 / dc)
        dev = [x - mu for x in xs]
        sq = dev[0] * dev[0]
        for d in dev[1:]:
            sq = sq + d * d
        var = jnp.sum(sq, axis=-1, keepdims=True) * (1.0 / dc)
        inv = lax.rsqrt(var + LN_EPS)
        for c in range(n_ca):
            lanes = slice(c * LANES, (c + 1) * LANES)
            y = dev[c] * inv * lg_ref[:, lanes] + lb_ref[:, lanes]
            act_a[slot, rws, lanes] = (y * _sigmoid(y)).astype(BF16)

        for c in range(n_cs):
            lanes = slice(c * LANES, (c + 1) * LANES)
            acc = None
            for k in range(k_short):
                start = r0 + hist_s - (k_short - 1) + k
                term = sw_ref[k:k + 1, lanes] * cv_ext[c, start:start + rows, :]
                acc = term if acc is None else acc + term
            bgate = pu_ref[rws, o_bg + c * LANES:o_bg + (c + 1) * LANES]
            act_s[slot, rws, lanes] = (bgate.astype(F32) * acc).astype(BF16)

    def stage2(n0):
        cols = slice(n0, n0 + n_chunk)
        ga_ref, gb_ref = (ga0_ref, gb0_ref) if n0 < half else (ga1_ref, gb1_ref)
        gcols = slice(n0 % half, n0 % half + n_chunk)
        ya = jnp.dot(act_a[prev], pw_ref[:, cols], preferred_element_type=F32)
        yb = jnp.dot(act_s[prev], wo_ref[:, cols], preferred_element_type=F32)
        ga = _sigmoid(ga_ref[:, gcols].astype(F32) + bg_ref[:, cols])
        gb = _sigmoid(gb_ref[:, gcols].astype(F32) + bg_ref[:, dm + n0:dm + n0 + n_chunk])
        m_ref[:, cols] = (ga * ya + gb * yb).astype(m_ref.dtype)

    for r in range(n_row_chunks):
        stage1(r * rows)
        stage2(r * n_chunk)

    for c in range(n_ca):
        a_ext[c, 0:hist_a, :] = a_ext[c, tt:tt + hist_a, :]
    for c in range(n_cs):
        cv_ext[c, 0:hist_s, :] = cv_ext[c, tt:tt + hist_s, :]


def _mixer(proj, meta_proj, b_gates, conf_dw_w, conf_dw_b, conf_ln_g, conf_ln_b,
           pw_bf16, short_dw_w, wout_bf16, *, batch, seq, dm, tt):
    dc = conf_dw_w.shape[1]
    ds = short_dw_w.shape[1]
    k_conf = conf_dw_w.shape[0]
    k_short = short_dw_w.shape[0]
    n_meta = meta_proj.shape[0]
    n_cols = proj.shape[1]
    u_cols = 2 * dc + 3 * ds
    half = dm // 2
    assert n_cols == u_cols + 2 * dm and u_cols % half == 0
    g_blk = u_cols // half
    hist_a = 2 * SUBLANES_BF16
    hist_s = SUBLANES_F32
    assert hist_a >= k_conf - 1 and hist_a >= n_meta and n_meta >= hist_s >= k_short - 1
    n_t = seq // tt
    n_blocks = batch * n_t
    kern = functools.partial(
        _mixer_kernel, tt=tt, n_t=n_t, dc=dc, ds=ds, dm=dm, k_conf=k_conf,
        k_short=k_short, n_meta=n_meta, hist_a=hist_a, hist_s=hist_s,
        rows=2 * SUBLANES_BF16)
    cur = lambda s: jnp.minimum(s, n_blocks - 1)
    lag = lambda s: jnp.maximum(s - 1, 0)
    const = lambda shape: pl.BlockSpec(shape, lambda s: (0, 0))
    gate = lambda k: pl.BlockSpec((tt, half), lambda s: (lag(s), g_blk + k))
    return pl.pallas_call(
        kern,
        grid=(n_blocks + 1,),
        in_specs=[
            pl.BlockSpec((tt, u_cols), lambda s: (cur(s), 0)),
            gate(0), gate(1), gate(2), gate(3),
            const((n_meta, n_cols)),
            const((1, 2 * dm)),
            const((k_conf, dc)),
            const((1, dc)),
            const((1, dc)),
            const((1, dc)),
            const((dc, dm)),
            const((k_short, ds)),
            const((ds, dm)),
        ],
        out_specs=pl.BlockSpec((tt, dm), lambda s: (lag(s), 0)),
        out_shape=jax.ShapeDtypeStruct((batch * seq, dm), BF16),
        scratch_shapes=[
            pltpu.VMEM((dc // LANES, hist_a + tt, LANES), F32),
            pltpu.VMEM((ds // LANES, hist_s + tt, LANES), F32),
            pltpu.VMEM((2, tt, dc), BF16),
            pltpu.VMEM((2, tt, ds), BF16),
        ],
        compiler_params=pltpu.CompilerParams(
            dimension_semantics=("arbitrary",),
            vmem_limit_bytes=VMEM_LIMIT_BYTES),
        name="mixer",
    )(proj, proj, proj, proj, proj, meta_proj, b_gates, conf_dw_w, conf_dw_b,
      conf_ln_g, conf_ln_b, pw_bf16, short_dw_w, wout_bf16)


def _mix_out_kernel(m_ref, w_ref, x_ref, g_ref, o_ref):
    mix = jnp.dot(m_ref[...], w_ref[...], preferred_element_type=F32)
    inv = lax.rsqrt(jnp.mean(mix * mix, axis=-1, keepdims=True) + RMS_EPS)
    o_ref[...] = x_ref[...] + mix * inv * g_ref[...]


def _mix_out(m, wo_bf16, x2d, gain, *, tm):
    rows, d = x2d.shape
    return pl.pallas_call(
        _mix_out_kernel,
        grid=(rows // tm,),
        in_specs=[
            pl.BlockSpec((tm, d), lambda i: (i, 0)),
            pl.BlockSpec((d, d), lambda i: (0, 0)),
            pl.BlockSpec((tm, d), lambda i: (i, 0)),
            pl.BlockSpec((1, d), lambda i: (0, 0)),
        ],
        out_specs=pl.BlockSpec((tm, d), lambda i: (i, 0)),
        out_shape=jax.ShapeDtypeStruct((rows, d), F32),
        compiler_params=pltpu.CompilerParams(
            dimension_semantics=("parallel",),
            vmem_limit_bytes=VMEM_LIMIT_BYTES),
        name="mix_out",
    )(m, wo_bf16, x2d, gain)


def _mlp_kernel(h_ref, gpre_ref, wup_ref, wdn_ref, gpost_ref, o_ref, n_ref, *,
                n_chunk):
    j = pl.program_id(1)

    @pl.when(j == 0)
    def _normalise():
        h = h_ref[...]
        inv = lax.rsqrt(jnp.mean(h * h, axis=-1, keepdims=True) + RMS_EPS)
        n_ref[...] = (h * inv * gpre_ref[...]).astype(BF16)

        o_ref[...] = jnp.zeros_like(o_ref)

    u = jnp.dot(n_ref[...], wup_ref[...], preferred_element_type=F32)
    u = jnp.maximum(u, 0.0)
    act = (u * u).astype(BF16)
    for n0 in range(0, o_ref.shape[1], n_chunk):
        cols = slice(n0, n0 + n_chunk)
        o_ref[:, cols] += jnp.dot(act, wdn_ref[:, cols], preferred_element_type=F32)

    @pl.when(j == pl.num_programs(1) - 1)
    def _finish():
        f = o_ref[...]
        inv = lax.rsqrt(jnp.mean(f * f, axis=-1, keepdims=True) + RMS_EPS)
        o_ref[...] = h_ref[...] + f * inv * gpost_ref[...]


def _mlp(h1, g_pre, wup_bf16, wdn_bf16, g_post, *, tm, tf):
    rows, d = h1.shape
    f = wup_bf16.shape[1]
    return pl.pallas_call(
        functools.partial(_mlp_kernel, n_chunk=4 * LANES),
        grid=(rows // tm, f // tf),
        in_specs=[
            pl.BlockSpec((tm, d), lambda i, j: (i, 0)),
            pl.BlockSpec((1, d), lambda i, j: (0, 0)),
            pl.BlockSpec((d, tf), lambda i, j: (0, j)),
            pl.BlockSpec((tf, d), lambda i, j: (j, 0)),
            pl.BlockSpec((1, d), lambda i, j: (0, 0)),
        ],
        out_specs=pl.BlockSpec((tm, d), lambda i, j: (i, 0)),
        out_shape=jax.ShapeDtypeStruct((rows, d), F32),
        scratch_shapes=[pltpu.VMEM((tm, d), BF16)],
        compiler_params=pltpu.CompilerParams(
            dimension_semantics=("parallel", "arbitrary"),
            vmem_limit_bytes=VMEM_LIMIT_BYTES),
        name="mlp",
    )(h1, g_pre, wup_bf16, wdn_bf16, g_post)


def kernel(x, meta, g_pre_mix, w_in, b_gates, conf_dw_w, conf_dw_b, conf_ln_g,
           conf_ln_b, conf_w_pw, short_dw_w, short_w_out, w_o, g_post_mix,
           g_pre_mlp, w_up, w_down, g_post_mlp):
    batch, seq, dm = x.shape
    depth = w_in.shape[0]
    n_meta = meta.shape[0]
    assert depth == 1, "meta-token history seeding assumes a single layer"

    x2d = x.reshape(batch * seq, dm)
    l = 0
    row = lambda v: v[l].reshape(1, -1)
    w_in_b = w_in[l].astype(BF16)
    in_cols = w_in_b.shape[1]

    proj = _rms_in_proj(x2d, row(g_pre_mix), w_in_b, tm=1024, tn=1024)
    meta_proj = _rms_in_proj(meta.astype(x.dtype), row(g_pre_mix), w_in_b,
                             tm=n_meta, tn=in_cols // 3)
    m = _mixer(proj, meta_proj, row(b_gates), conf_dw_w[l], row(conf_dw_b),
               row(conf_ln_g), row(conf_ln_b), conf_w_pw[l].astype(BF16),
               short_dw_w[l], short_w_out[l].astype(BF16),
               batch=batch, seq=seq, dm=dm, tt=256)
    h1 = _mix_out(m, w_o[l].astype(BF16), x2d, row(g_post_mix), tm=512)
    out = _mlp(h1, row(g_pre_mlp), w_up[l].astype(BF16), w_down[l].astype(BF16),
               row(g_post_mlp), tm=1024, tf=512)
    return out.reshape(batch, seq, dm)
```

```python
import functools

import jax
import jax.numpy as jnp
from jax import lax
from jax.experimental import pallas as pl
from jax.experimental.pallas import tpu as pltpu

RMS_EPS = 1e-6
LN_EPS = 1e-5

LANES = 128
SUBLANES_F32 = 8
SUBLANES_BF16 = 16
VMEM_LIMIT_BYTES = 58 * 1024 * 1024

BF16 = jnp.bfloat16
F32 = jnp.float32


def _sigmoid(x):
    return 1.0 / (1.0 + jnp.exp(-x))


def _rms_in_proj_kernel(x_ref, g_ref, w_ref, o_ref, n_ref):
    @pl.when(pl.program_id(1) == 0)
    def _normalise():
        x = x_ref[...]
        inv = lax.rsqrt(jnp.mean(x * x, axis=-1, keepdims=True) + RMS_EPS)
        n_ref[...] = (x * inv * g_ref[...]).astype(BF16)

    o_ref[...] = jnp.dot(n_ref[...], w_ref[...],
                         preferred_element_type=F32).astype(o_ref.dtype)


def _rms_in_proj(x2d, gain, w_bf16, *, tm, tn):
    m, d = x2d.shape
    n = w_bf16.shape[1]
    return pl.pallas_call(
        _rms_in_proj_kernel,
        grid=(m // tm, n // tn),
        in_specs=[
            pl.BlockSpec((tm, d), lambda i, j: (i, 0)),
            pl.BlockSpec((1, d), lambda i, j: (0, 0)),
            pl.BlockSpec((d, tn), lambda i, j: (0, j)),
        ],
        out_specs=pl.BlockSpec((tm, tn), lambda i, j: (i, j)),
        out_shape=jax.ShapeDtypeStruct((m, n), BF16),
        scratch_shapes=[pltpu.VMEM((tm, d), BF16)],
        compiler_params=pltpu.CompilerParams(
            dimension_semantics=("parallel", "arbitrary"),
            vmem_limit_bytes=VMEM_LIMIT_BYTES),
        name="rms_in_proj",
    )(x2d, gain, w_bf16)


def _mixer_kernel(pu_ref, ga0_ref, ga1_ref, gb0_ref, gb1_ref, x_ref, meta_ref, bg_ref,
                  cw_ref, cb_ref, lg_ref, lb_ref, pw_ref, sw_ref, wsb_ref, wo_ref,
                  gpost_ref, h1_ref,
                  a_ext, cv_ext, act_a, act_s, m_buf, mix_buf, *,
                  tt, n_t, dc, ds, dm, k_conf, k_short, n_meta, hist_a, hist_s, rows):
    o_ag = dc
    o_bg = 2 * dc
    o_cg = o_bg + ds
    o_v = o_cg + ds
    n_ca = dc // LANES
    n_cs = ds // LANES
    n_row_chunks = tt // rows
    n_chunk = dm // n_row_chunks
    half = dm // 2

    step = pl.program_id(0)
    a_new = lax.rem(step, 2)
    a_old = 1 - a_new
    new, old = 0, 1

    def glu(val, gate):
        return val.astype(F32) * _sigmoid(gate.astype(F32))

    @pl.when(lax.rem(step, n_t) == 0)
    def _seed_history():
        mp = meta_ref[...]
        a_meta = glu(mp[:, 0:dc], mp[:, o_ag:o_ag + dc])
        cv_meta = mp[:, o_cg:o_cg + ds].astype(F32) * mp[:, o_v:o_v + ds].astype(F32)
        for c in range(n_ca):
            a_ext[c, 0:hist_a - n_meta, :] = jnp.zeros((hist_a - n_meta, LANES), F32)
            a_ext[c, hist_a - n_meta:hist_a, :] = a_meta[:, c * LANES:(c + 1) * LANES]
        for c in range(n_cs):
            cv_ext[c, 0:hist_s, :] = cv_meta[n_meta - hist_s:, c * LANES:(c + 1) * LANES]

    @pl.when(step == 0)
    def _no_previous_blocks():
        act_a[1] = jnp.zeros(act_a.shape[1:], BF16)
        act_s[1] = jnp.zeros(act_s.shape[1:], BF16)
        m_buf[old] = jnp.zeros(m_buf.shape[1:], BF16)

    def stage1(r0):
        rws = slice(r0, r0 + rows)
        a = glu(pu_ref[rws, 0:dc], pu_ref[rws, o_ag:o_ag + dc])
        cv = pu_ref[rws, o_cg:o_cg + ds].astype(F32) * pu_ref[rws, o_v:o_v + ds].astype(F32)
        for c in range(n_ca):
            a_ext[c, hist_a + r0:hist_a + r0 + rows, :] = a[:, c * LANES:(c + 1) * LANES]
        for c in range(n_cs):
            cv_ext[c, hist_s + r0:hist_s + r0 + rows, :] = cv[:, c * LANES:(c + 1) * LANES]

        xs = []
        for c in range(n_ca):
            lanes = slice(c * LANES, (c + 1) * LANES)
            acc = jnp.broadcast_to(cb_ref[:, lanes], (rows, LANES))
            for k in range(k_conf):
                start = r0 + hist_a - (k_conf - 1) + k
                acc = acc + cw_ref[k:k + 1, lanes] * a_ext[c, start:start + rows, :]
            xs.append(acc)
        tot = xs[0]
        for x in xs[1:]:
            tot = tot + x
        mu = jnp.sum(tot, axis=-1, keepdims=True) * (1.0 / dc)
        dev = [x - mu for x in xs]
        sq = dev[0] * dev[0]
        for d in dev[1:]:
            sq = sq + d * d
        var = jnp.sum(sq, axis=-1, keepdims=True) * (1.0 / dc)
        inv = lax.rsqrt(var + LN_EPS)
        for c in range(n_ca):
            lanes = slice(c * LANES, (c + 1) * LANES)
            y = dev[c] * inv * lg_ref[:, lanes] + lb_ref[:, lanes]
            act_a[a_new, rws, lanes] =(y * _sigmoid(y)).astype(BF16)

        for c in range(n_cs):
            lanes = slice(c * LANES, (c + 1) * LANES)
            acc = None
            for k in range(k_short):
                start = r0 + hist_s - (k_short - 1) + k
                term = sw_ref[k:k + 1, lanes] * cv_ext[c, start:start + rows, :]
                acc = term if acc is None else acc + term
            bgate = pu_ref[rws, o_bg + c * LANES:o_bg + (c + 1) * LANES]
            act_s[a_new, rws, lanes] =(bgate.astype(F32) * acc).astype(BF16)

    def stage2(n0):
        cols = slice(n0, n0 + n_chunk)
        ga_ref, gb_ref = (ga0_ref, gb0_ref) if n0 < half else (ga1_ref, gb1_ref)
        gcols = slice(n0 % half, n0 % half + n_chunk)
        ya = jnp.dot(act_a[a_old], pw_ref[:, cols], preferred_element_type=F32)
        yb = jnp.dot(act_s[a_old], wsb_ref[:, cols], preferred_element_type=F32)
        ga = _sigmoid(ga_ref[:, gcols].astype(F32) + bg_ref[:, cols])
        gb = _sigmoid(gb_ref[:, gcols].astype(F32) + bg_ref[:, dm + n0:dm + n0 + n_chunk])
        m_buf[new, :, cols] = (ga * ya + gb * yb).astype(BF16)

    def stage3(n0):
        cols = slice(n0, n0 + n_chunk)
        mix_buf[:, cols] = jnp.dot(m_buf[old], wo_ref[:, cols],
                                   preferred_element_type=F32)

    for r in range(n_row_chunks):
        stage1(r * rows)
        stage3(r * n_chunk)
        stage2(r * n_chunk)

    for r in range(n_row_chunks):
        rws = slice(r * rows, (r + 1) * rows)
        mix = mix_buf[rws, :]
        inv = lax.rsqrt(jnp.mean(mix * mix, axis=-1, keepdims=True) + RMS_EPS)
        h1_ref[rws, :] = x_ref[rws, :] + mix * inv * gpost_ref[...]

    m_buf[old] = m_buf[new]

    for c in range(n_ca):
        a_ext[c, 0:hist_a, :] = a_ext[c, tt:tt + hist_a, :]
    for c in range(n_cs):
        cv_ext[c, 0:hist_s, :] = cv_ext[c, tt:tt + hist_s, :]


def _mixer(proj, x2d, meta_proj, b_gates, conf_dw_w, conf_dw_b, conf_ln_g, conf_ln_b,
           pw_bf16, short_dw_w, wsb_bf16, wo_bf16, g_post, *, batch, seq, tt):
    dm = x2d.shape[1]
    dc = conf_dw_w.shape[1]
    ds = short_dw_w.shape[1]
    k_conf = conf_dw_w.shape[0]
    k_short = short_dw_w.shape[0]
    n_meta = meta_proj.shape[0]
    n_cols = proj.shape[1]
    u_cols = 2 * dc + 3 * ds
    half = dm // 2
    assert n_cols == u_cols + 2 * dm and u_cols % half == 0
    g_blk = u_cols // half
    hist_a = 2 * SUBLANES_BF16
    hist_s = SUBLANES_F32
    assert hist_a >= k_conf - 1 and hist_a >= n_meta and n_meta >= hist_s >= k_short - 1
    n_t = seq // tt
    n_blocks = batch * n_t
    kern = functools.partial(
        _mixer_kernel, tt=tt, n_t=n_t, dc=dc, ds=ds, dm=dm, k_conf=k_conf,
        k_short=k_short, n_meta=n_meta, hist_a=hist_a, hist_s=hist_s,
        rows=2 * SUBLANES_BF16)
    stage_blk = lambda s, lag: jnp.clip(s - lag, 0, n_blocks - 1)
    const = lambda shape: pl.BlockSpec(shape, lambda s: (0, 0))
    gate = lambda k: pl.BlockSpec((tt, half), lambda s: (stage_blk(s, 1), g_blk + k))
    return pl.pallas_call(
        kern,
        grid=(n_blocks + 2,),
        in_specs=[
            pl.BlockSpec((tt, u_cols), lambda s: (stage_blk(s, 0), 0)),
            gate(0), gate(1), gate(2), gate(3),
            pl.BlockSpec((tt, dm), lambda s: (stage_blk(s, 2), 0)),
            const((n_meta, n_cols)),
            const((1, 2 * dm)),
            const((k_conf, dc)),
            const((1, dc)),
            const((1, dc)),
            const((1, dc)),
            const((dc, dm)),
            const((k_short, ds)),
            const((ds, dm)),
            const((dm, dm)),
            const((1, dm)),
        ],
        out_specs=pl.BlockSpec((tt, dm), lambda s: (stage_blk(s, 2), 0)),
        out_shape=jax.ShapeDtypeStruct((batch * seq, dm), F32),
        scratch_shapes=[
            pltpu.VMEM((dc // LANES, hist_a + tt, LANES), F32),
            pltpu.VMEM((ds // LANES, hist_s + tt, LANES), F32),
            pltpu.VMEM((2, tt, dc), BF16),
            pltpu.VMEM((2, tt, ds), BF16),
            pltpu.VMEM((2, tt, dm), BF16),
            pltpu.VMEM((tt, dm), F32),
        ],
        compiler_params=pltpu.CompilerParams(
            dimension_semantics=("arbitrary",),
            vmem_limit_bytes=VMEM_LIMIT_BYTES),
        name="mixer",
    )(proj, proj, proj, proj, proj, x2d, meta_proj, b_gates, conf_dw_w, conf_dw_b,
      conf_ln_g, conf_ln_b, pw_bf16, short_dw_w, wsb_bf16, wo_bf16, g_post)


def _mlp_kernel(h_ref, gpre_ref, wup_ref, wdn_ref, gpost_ref, o_ref, n_ref, *,
                n_chunk):
    j = pl.program_id(1)

    @pl.when(j == 0)
    def _normalise():
        h = h_ref[...]
        inv = lax.rsqrt(jnp.mean(h * h, axis=-1, keepdims=True) + RMS_EPS)
        n_ref[...] = (h * inv * gpre_ref[...]).astype(BF16)

        o_ref[...] = jnp.zeros_like(o_ref)

    u = jnp.dot(n_ref[...], wup_ref[...], preferred_element_type=F32)
    u = jnp.maximum(u, 0.0)
    act = (u * u).astype(BF16)
    for n0 in range(0, o_ref.shape[1], n_chunk):
        cols = slice(n0, n0 + n_chunk)
        o_ref[:, cols] += jnp.dot(act, wdn_ref[:, cols], preferred_element_type=F32)

    @pl.when(j == pl.num_programs(1) - 1)
    def _finish():
        f = o_ref[...]
        inv = lax.rsqrt(jnp.mean(f * f, axis=-1, keepdims=True) + RMS_EPS)
        o_ref[...] = h_ref[...] + f * inv * gpost_ref[...]


def _mlp(h1, g_pre, wup_bf16, wdn_bf16, g_post, *, tm, tf):
    rows, d = h1.shape
    f = wup_bf16.shape[1]
    return pl.pallas_call(
        functools.partial(_mlp_kernel, n_chunk=4 * LANES),
        grid=(rows // tm, f // tf),
        in_specs=[
            pl.BlockSpec((tm, d), lambda i, j: (i, 0)),
            pl.BlockSpec((1, d), lambda i, j: (0, 0)),
            pl.BlockSpec((d, tf), lambda i, j: (0, j)),
            pl.BlockSpec((tf, d), lambda i, j: (j, 0)),
            pl.BlockSpec((1, d), lambda i, j: (0, 0)),
        ],
        out_specs=pl.BlockSpec((tm, d), lambda i, j: (i, 0)),
        out_shape=jax.ShapeDtypeStruct((rows, d), F32),
        scratch_shapes=[pltpu.VMEM((tm, d), BF16)],
        compiler_params=pltpu.CompilerParams(
            dimension_semantics=("parallel", "arbitrary"),
            vmem_limit_bytes=VMEM_LIMIT_BYTES),
        name="mlp",
    )(h1, g_pre, wup_bf16, wdn_bf16, g_post)


def kernel(x, meta, g_pre_mix, w_in, b_gates, conf_dw_w, conf_dw_b, conf_ln_g,
           conf_ln_b, conf_w_pw, short_dw_w, short_w_out, w_o, g_post_mix,
           g_pre_mlp, w_up, w_down, g_post_mlp):
    batch, seq, dm = x.shape
    depth = w_in.shape[0]
    n_meta = meta.shape[0]
    assert depth == 1, "meta-token history seeding assumes a single layer"

    x2d = x.reshape(batch * seq, dm)
    l = 0
    row = lambda v: v[l].reshape(1, -1)
    w_in_b = w_in[l].astype(BF16)
    in_cols = w_in_b.shape[1]

    proj = _rms_in_proj(x2d, row(g_pre_mix), w_in_b, tm=1024, tn=1024)
    meta_proj = _rms_in_proj(meta.astype(x.dtype), row(g_pre_mix), w_in_b,
                             tm=n_meta, tn=in_cols // 3)
    h1 = _mixer(proj, x2d, meta_proj, row(b_gates), conf_dw_w[l], row(conf_dw_b),
                row(conf_ln_g), row(conf_ln_b), conf_w_pw[l].astype(BF16),
                short_dw_w[l], short_w_out[l].astype(BF16), w_o[l].astype(BF16),
                row(g_post_mix), batch=batch, seq=seq, tt=256)
    out = _mlp(h1, row(g_pre_mlp), w_up[l].astype(BF16), w_down[l].astype(BF16),
               row(g_post_mlp), tm=1024, tf=512)
    return out.reshape(batch, seq, dm)
```

```python
import functools

import jax
import jax.numpy as jnp
from jax import lax
from jax.experimental import pallas as pl
from jax.experimental.pallas import tpu as pltpu

RMS_EPS = 1e-6
LN_EPS = 1e-5

LANES = 128
SUBLANES_F32 = 8
SUBLANES_BF16 = 16
VMEM_LIMIT_BYTES = 58 * 1024 * 1024

BF16 = jnp.bfloat16
F32 = jnp.float32


def _sigmoid(x):
    return 1.0 / (1.0 + jnp.exp(-x))


def _rms_in_proj_kernel(x_ref, g_ref, w_ref, *rest, n_side):
    side_in = rest[:n_side]
    o_ref = rest[n_side]
    side_out = rest[n_side + 1:2 * n_side + 1]
    n_ref = rest[2 * n_side + 1]

    @pl.when(pl.program_id(1) == 0)
    def _normalise():
        x = x_ref[...]
        inv = lax.rsqrt(jnp.mean(x * x, axis=-1, keepdims=True) + RMS_EPS)
        n_ref[...] = (x * inv * g_ref[...]).astype(BF16)

    o_ref[...] = jnp.dot(n_ref[...], w_ref[...],
                         preferred_element_type=F32).astype(o_ref.dtype)
    for src_ref, dst_ref in zip(side_in, side_out):
        dst_ref[...] = src_ref[...].astype(dst_ref.dtype)


def _rms_in_proj(x2d, gain, w_bf16, side_weights=(), *, tm, tn):
    m, d = x2d.shape
    n = w_bf16.shape[1]
    n_j = n // tn
    n_steps = (m // tm) * n_j
    side_specs = []
    for w in side_weights:
        n_blk = n_steps
        while w.shape[0] % n_blk or (w.shape[0] // n_blk) % SUBLANES_BF16:
            n_blk -= 1
        side_specs.append(pl.BlockSpec(
            (w.shape[0] // n_blk, w.shape[1]),
            lambda i, j, n_blk=n_blk: (jnp.minimum(i * n_j + j, n_blk - 1), 0)))
    outs = pl.pallas_call(
        functools.partial(_rms_in_proj_kernel, n_side=len(side_weights)),
        grid=(m // tm, n_j),
        in_specs=[
            pl.BlockSpec((tm, d), lambda i, j: (i, 0)),
            pl.BlockSpec((1, d), lambda i, j: (0, 0)),
            pl.BlockSpec((d, tn), lambda i, j: (0, j)),
            *side_specs,
        ],
        out_specs=[pl.BlockSpec((tm, tn), lambda i, j: (i, j)), *side_specs],
        out_shape=[jax.ShapeDtypeStruct((m, n), BF16),
                   *(jax.ShapeDtypeStruct(w.shape, BF16) for w in side_weights)],
        scratch_shapes=[pltpu.VMEM((tm, d), BF16)],
        compiler_params=pltpu.CompilerParams(
            dimension_semantics=("arbitrary", "arbitrary"),
            vmem_limit_bytes=VMEM_LIMIT_BYTES),
        name="rms_in_proj",
    )(x2d, gain, w_bf16, *side_weights)
    return outs[0], list(outs[1:])


def _mixer_kernel(pu_ref, ga0_ref, ga1_ref, gb0_ref, gb1_ref, x_ref, meta_ref, bg_ref,
                  cw_ref, cb_ref, lg_ref, lb_ref, pw_ref, sw_ref, wsb_ref, wo_ref,
                  gpost_ref, h1_ref,
                  a_ext, cv_ext, act_a, act_s, m_buf, mix_buf, *,
                  tt, n_t, dc, ds, dm, k_conf, k_short, n_meta, hist_a, hist_s, rows):
    o_ag = dc
    o_bg = 2 * dc
    o_cg = o_bg + ds
    o_v = o_cg + ds
    n_ca = dc // LANES
    n_cs = ds // LANES
    n_row_chunks = tt // rows
    n_chunk = dm // n_row_chunks
    half = dm // 2

    step = pl.program_id(0)
    a_new = lax.rem(step, 2)
    a_old = 1 - a_new
    new, old = 0, 1

    def glu(val, gate):
        return val.astype(F32) * _sigmoid(gate.astype(F32))

    @pl.when(lax.rem(step, n_t) == 0)
    def _seed_history():
        mp = meta_ref[...]
        a_meta = glu(mp[:, 0:dc], mp[:, o_ag:o_ag + dc])
        cv_meta = mp[:, o_cg:o_cg + ds].astype(F32) * mp[:, o_v:o_v + ds].astype(F32)
        for c in range(n_ca):
            a_ext[c, 0:hist_a - n_meta, :] = jnp.zeros((hist_a - n_meta, LANES), F32)
            a_ext[c, hist_a - n_meta:hist_a, :] = a_meta[:, c * LANES:(c + 1) * LANES]
        for c in range(n_cs):
            cv_ext[c, 0:hist_s, :] = cv_meta[n_meta - hist_s:, c * LANES:(c + 1) * LANES]

    @pl.when(step == 0)
    def _no_previous_blocks():
        act_a[1] = jnp.zeros(act_a.shape[1:], BF16)
        act_s[1] = jnp.zeros(act_s.shape[1:], BF16)
        m_buf[old] = jnp.zeros(m_buf.shape[1:], BF16)

    def stage1(r0):
        rws = slice(r0, r0 + rows)
        a = glu(pu_ref[rws, 0:dc], pu_ref[rws, o_ag:o_ag + dc])
        cv = pu_ref[rws, o_cg:o_cg + ds].astype(F32) * pu_ref[rws, o_v:o_v + ds].astype(F32)
        for c in range(n_ca):
            a_ext[c, hist_a + r0:hist_a + r0 + rows, :] = a[:, c * LANES:(c + 1) * LANES]
        for c in range(n_cs):
            cv_ext[c, hist_s + r0:hist_s + r0 + rows, :] = cv[:, c * LANES:(c + 1) * LANES]

        xs = []
        for c in range(n_ca):
            lanes = slice(c * LANES, (c + 1) * LANES)
            acc = jnp.broadcast_to(cb_ref[:, lanes], (rows, LANES))
            for k in range(k_conf):
                start = r0 + hist_a - (k_conf - 1) + k
                acc = acc + cw_ref[k:k + 1, lanes] * a_ext[c, start:start + rows, :]
            xs.append(acc)
        tot = xs[0]
        for x in xs[1:]:
            tot = tot + x
        mu = jnp.sum(tot, axis=-1, keepdims=True) * (1.0 / dc)
        dev = [x - mu for x in xs]
        sq = dev[0] * dev[0]
        for d in dev[1:]:
            sq = sq + d * d
        var = jnp.sum(sq, axis=-1, keepdims=True) * (1.0 / dc)
        inv = lax.rsqrt(var + LN_EPS)
        for c in range(n_ca):
            lanes = slice(c * LANES, (c + 1) * LANES)
            y = dev[c] * inv * lg_ref[:, lanes] + lb_ref[:, lanes]
            act_a[a_new, rws, lanes] =(y * _sigmoid(y)).astype(BF16)

        for c in range(n_cs):
            lanes = slice(c * LANES, (c + 1) * LANES)
            acc = None
            for k in range(k_short):
                start = r0 + hist_s - (k_short - 1) + k
                term = sw_ref[k:k + 1, lanes] * cv_ext[c, start:start + rows, :]
                acc = term if acc is None else acc + term
            bgate = pu_ref[rws, o_bg + c * LANES:o_bg + (c + 1) * LANES]
            act_s[a_new, rws, lanes] =(bgate.astype(F32) * acc).astype(BF16)

    def stage2(n0):
        cols = slice(n0, n0 + n_chunk)
        ga_ref, gb_ref = (ga0_ref, gb0_ref) if n0 < half else (ga1_ref, gb1_ref)
        gcols = slice(n0 % half, n0 % half + n_chunk)
        ya = jnp.dot(act_a[a_old], pw_ref[:, cols], preferred_element_type=F32)
        yb = jnp.dot(act_s[a_old], wsb_ref[:, cols], preferred_element_type=F32)
        ga = _sigmoid(ga_ref[:, gcols].astype(F32) + bg_ref[:, cols])
        gb = _sigmoid(gb_ref[:, gcols].astype(F32) + bg_ref[:, dm + n0:dm + n0 + n_chunk])
        m_buf[new, :, cols] = (ga * ya + gb * yb).astype(BF16)

    def stage3(n0):
        cols = slice(n0, n0 + n_chunk)
        mix_buf[:, cols] = jnp.dot(m_buf[old], wo_ref[:, cols],
                                   preferred_element_type=F32)

    for r in range(n_row_chunks):
        stage1(r * rows)
        stage3(r * n_chunk)
        stage2(r * n_chunk)

    for r in range(n_row_chunks):
        rws = slice(r * rows, (r + 1) * rows)
        mix = mix_buf[rws, :]
        inv = lax.rsqrt(jnp.mean(mix * mix, axis=-1, keepdims=True) + RMS_EPS)
        h1_ref[rws, :] = x_ref[rws, :] + mix * inv * gpost_ref[...]

    m_buf[old] = m_buf[new]

    for c in range(n_ca):
        a_ext[c, 0:hist_a, :] = a_ext[c, tt:tt + hist_a, :]
    for c in range(n_cs):
        cv_ext[c, 0:hist_s, :] = cv_ext[c, tt:tt + hist_s, :]


def _mixer(proj, x2d, meta_proj, b_gates, conf_dw_w, conf_dw_b, conf_ln_g, conf_ln_b,
           pw_bf16, short_dw_w, wsb_bf16, wo_bf16, g_post, *, batch, seq, tt):
    dm = x2d.shape[1]
    dc = conf_dw_w.shape[1]
    ds = short_dw_w.shape[1]
    k_conf = conf_dw_w.shape[0]
    k_short = short_dw_w.shape[0]
    n_meta = meta_proj.shape[0]
    n_cols = proj.shape[1]
    u_cols = 2 * dc + 3 * ds
    half = dm // 2
    assert n_cols == u_cols + 2 * dm and u_cols % half == 0
    g_blk = u_cols // half
    hist_a = 2 * SUBLANES_BF16
    hist_s = SUBLANES_F32
    assert hist_a >= k_conf - 1 and hist_a >= n_meta and n_meta >= hist_s >= k_short - 1
    n_t = seq // tt
    n_blocks = batch * n_t
    kern = functools.partial(
        _mixer_kernel, tt=tt, n_t=n_t, dc=dc, ds=ds, dm=dm, k_conf=k_conf,
        k_short=k_short, n_meta=n_meta, hist_a=hist_a, hist_s=hist_s,
        rows=2 * SUBLANES_BF16)
    stage_blk = lambda s, lag: jnp.clip(s - lag, 0, n_blocks - 1)
    const = lambda shape: pl.BlockSpec(shape, lambda s: (0, 0))
    gate = lambda k: pl.BlockSpec((tt, half), lambda s: (stage_blk(s, 1), g_blk + k))
    return pl.pallas_call(
        kern,
        grid=(n_blocks + 2,),
        in_specs=[
            pl.BlockSpec((tt, u_cols), lambda s: (stage_blk(s, 0), 0)),
            gate(0), gate(1), gate(2), gate(3),
            pl.BlockSpec((tt, dm), lambda s: (stage_blk(s, 2), 0)),
            const((n_meta, n_cols)),
            const((1, 2 * dm)),
            const((k_conf, dc)),
            const((1, dc)),
            const((1, dc)),
            const((1, dc)),
            const((dc, dm)),
            const((k_short, ds)),
            const((ds, dm)),
            const((dm, dm)),
            const((1, dm)),
        ],
        out_specs=pl.BlockSpec((tt, dm), lambda s: (stage_blk(s, 2), 0)),
        out_shape=jax.ShapeDtypeStruct((batch * seq, dm), F32),
        scratch_shapes=[
            pltpu.VMEM((dc // LANES, hist_a + tt, LANES), F32),
            pltpu.VMEM((ds // LANES, hist_s + tt, LANES), F32),
            pltpu.VMEM((2, tt, dc), BF16),
            pltpu.VMEM((2, tt, ds), BF16),
            pltpu.VMEM((2, tt, dm), BF16),
            pltpu.VMEM((tt, dm), F32),
        ],
        compiler_params=pltpu.CompilerParams(
            dimension_semantics=("arbitrary",),
            vmem_limit_bytes=VMEM_LIMIT_BYTES),
        name="mixer",
    )(proj, proj, proj, proj, proj, x2d, meta_proj, b_gates, conf_dw_w, conf_dw_b,
      conf_ln_g, conf_ln_b, pw_bf16, short_dw_w, wsb_bf16, wo_bf16, g_post)


def _mlp_kernel(h_ref, gpre_ref, wup_ref, wdn_ref, gpost_ref, o_ref, n_ref, *,
                n_chunk):
    j = pl.program_id(1)

    @pl.when(j == 0)
    def _normalise():
        h = h_ref[...]
        inv = lax.rsqrt(jnp.mean(h * h, axis=-1, keepdims=True) + RMS_EPS)
        n_ref[...] = (h * inv * gpre_ref[...]).astype(BF16)

        o_ref[...] = jnp.zeros_like(o_ref)

    u = jnp.dot(n_ref[...], wup_ref[...], preferred_element_type=F32)
    u = jnp.maximum(u, 0.0)
    act = (u * u).astype(BF16)
    for n0 in range(0, o_ref.shape[1], n_chunk):
        cols = slice(n0, n0 + n_chunk)
        o_ref[:, cols] += jnp.dot(act, wdn_ref[:, cols], preferred_element_type=F32)

    @pl.when(j == pl.num_programs(1) - 1)
    def _finish():
        f = o_ref[...]
        inv = lax.rsqrt(jnp.mean(f * f, axis=-1, keepdims=True) + RMS_EPS)
        o_ref[...] = h_ref[...] + f * inv * gpost_ref[...]


def _mlp(h1, g_pre, wup_bf16, wdn_bf16, g_post, *, tm, tf):
    rows, d = h1.shape
    f = wup_bf16.shape[1]
    return pl.pallas_call(
        functools.partial(_mlp_kernel, n_chunk=4 * LANES),
        grid=(rows // tm, f // tf),
        in_specs=[
            pl.BlockSpec((tm, d), lambda i, j: (i, 0)),
            pl.BlockSpec((1, d), lambda i, j: (0, 0)),
            pl.BlockSpec((d, tf), lambda i, j: (0, j)),
            pl.BlockSpec((tf, d), lambda i, j: (j, 0)),
            pl.BlockSpec((1, d), lambda i, j: (0, 0)),
        ],
        out_specs=pl.BlockSpec((tm, d), lambda i, j: (i, 0)),
        out_shape=jax.ShapeDtypeStruct((rows, d), F32),
        scratch_shapes=[pltpu.VMEM((tm, d), BF16)],
        compiler_params=pltpu.CompilerParams(
            dimension_semantics=("parallel", "arbitrary"),
            vmem_limit_bytes=VMEM_LIMIT_BYTES),
        name="mlp",
    )(h1, g_pre, wup_bf16, wdn_bf16, g_post)


def kernel(x, meta, g_pre_mix, w_in, b_gates, conf_dw_w, conf_dw_b, conf_ln_g,
           conf_ln_b, conf_w_pw, short_dw_w, short_w_out, w_o, g_post_mix,
           g_pre_mlp, w_up, w_down, g_post_mlp):
    batch, seq, dm = x.shape
    depth = w_in.shape[0]
    n_meta = meta.shape[0]
    assert depth == 1, "meta-token history seeding assumes a single layer"

    x2d = x.reshape(batch * seq, dm)
    l = 0
    row = lambda v: v[l].reshape(1, -1)
    w_in_b = w_in[l].astype(BF16)
    in_cols = w_in_b.shape[1]

    proj, (w_pw_b, w_sb_b, w_o_b, w_up_b, w_down_b) = _rms_in_proj(
        x2d, row(g_pre_mix), w_in_b,
        side_weights=(conf_w_pw[l], short_w_out[l], w_o[l], w_up[l], w_down[l]),
        tm=1024, tn=1024)
    meta_proj, _ = _rms_in_proj(meta.astype(x.dtype), row(g_pre_mix), w_in_b,
                                tm=n_meta, tn=in_cols // 3)
    h1 = _mixer(proj, x2d, meta_proj, row(b_gates), conf_dw_w[l], row(conf_dw_b),
                row(conf_ln_g), row(conf_ln_b), w_pw_b, short_dw_w[l], w_sb_b, w_o_b,
                row(g_post_mix), batch=batch, seq=seq, tt=256)
    out = _mlp(h1, row(g_pre_mlp), w_up_b, w_down_b, row(g_post_mlp), tm=1024, tf=512)
    return out.reshape(batch, seq, dm)
```

```python
import functools

import jax
import jax.numpy as jnp
from jax import lax
from jax.experimental import pallas as pl
from jax.experimental.pallas import tpu as pltpu

RMS_EPS = 1e-6
LN_EPS = 1e-5

LANES = 128
SUBLANES_F32 = 8
SUBLANES_BF16 = 16
VMEM_LIMIT_BYTES = 58 * 1024 * 1024

BF16 = jnp.bfloat16
F32 = jnp.float32


def _sigmoid(x):
    return 1.0 / (1.0 + jnp.exp(-x))


def _rms_in_proj_kernel(x_ref, g_ref, w_ref, *rest, n_side):
    side_in = rest[:n_side]
    o_ref = rest[n_side]
    side_out = rest[n_side + 1:2 * n_side + 1]
    n_ref = rest[2 * n_side + 1]

    @pl.when(pl.program_id(1) == 0)
    def _normalise():
        x = x_ref[...]
        inv = lax.rsqrt(jnp.mean(x * x, axis=-1, keepdims=True) + RMS_EPS)
        n_ref[...] = (x * inv * g_ref[...]).astype(BF16)

    o_ref[...] = jnp.dot(n_ref[...], w_ref[...],
                         preferred_element_type=F32).astype(o_ref.dtype)
    for src_ref, dst_ref in zip(side_in, side_out):
        dst_ref[...] = src_ref[...].astype(dst_ref.dtype)


def _rms_in_proj(x2d, gain, w_bf16, side_weights=(), *, tm, tn):
    m, d = x2d.shape
    n = w_bf16.shape[1]
    n_j = n // tn
    n_steps = (m // tm) * n_j
    side_specs = []
    for w in side_weights:
        n_blk = n_steps
        while w.shape[0] % n_blk or (w.shape[0] // n_blk) % SUBLANES_BF16:
            n_blk -= 1
        side_specs.append(pl.BlockSpec(
            (w.shape[0] // n_blk, w.shape[1]),
            lambda i, j, n_blk=n_blk: (jnp.minimum(i * n_j + j, n_blk - 1), 0)))
    outs = pl.pallas_call(
        functools.partial(_rms_in_proj_kernel, n_side=len(side_weights)),
        grid=(m // tm, n_j),
        in_specs=[
            pl.BlockSpec((tm, d), lambda i, j: (i, 0)),
            pl.BlockSpec((1, d), lambda i, j: (0, 0)),
            pl.BlockSpec((d, tn), lambda i, j: (0, j)),
            *side_specs,
        ],
        out_specs=[pl.BlockSpec((tm, tn), lambda i, j: (i, j)), *side_specs],
        out_shape=[jax.ShapeDtypeStruct((m, n), BF16),
                   *(jax.ShapeDtypeStruct(w.shape, BF16) for w in side_weights)],
        scratch_shapes=[pltpu.VMEM((tm, d), BF16)],
        compiler_params=pltpu.CompilerParams(
            dimension_semantics=("arbitrary", "arbitrary"),
            vmem_limit_bytes=VMEM_LIMIT_BYTES),
        name="rms_in_proj",
    )(x2d, gain, w_bf16, *side_weights)
    return outs[0], list(outs[1:])


def _mixer_kernel(pu_ref, ga0_ref, ga1_ref, gb0_ref, gb1_ref, x_ref, meta_ref, bg_ref,
                  cw_ref, cb_ref, lg_ref, lb_ref, pw_ref, sw_ref, wsb_ref, wo_ref,
                  gpost_ref, h1_ref,
                  a_ext, cv_ext, act_a, act_s, m_buf, mix_buf, *,
                  tt, n_t, dc, ds, dm, k_conf, k_short, n_meta, hist_a, hist_s, rows):
    o_ag = dc
    o_bg = 2 * dc
    o_cg = o_bg + ds
    o_v = o_cg + ds
    n_ca = dc // LANES
    n_cs = ds // LANES
    n_row_chunks = tt // rows
    n_chunk = 2 * LANES
    n_mxu_chunks = dm // n_chunk
    assert n_row_chunks % n_mxu_chunks == 0
    rows_per_mxu_chunk = n_row_chunks // n_mxu_chunks
    half = dm // 2

    step = pl.program_id(0)
    a_new = lax.rem(step, 2)
    a_old = 1 - a_new
    new, old = 0, 1

    def glu(val, gate):
        return val.astype(F32) * _sigmoid(gate.astype(F32))

    @pl.when(lax.rem(step, n_t) == 0)
    def _seed_history():
        mp = meta_ref[...]
        a_meta = glu(mp[:, 0:dc], mp[:, o_ag:o_ag + dc])
        cv_meta = mp[:, o_cg:o_cg + ds].astype(F32) * mp[:, o_v:o_v + ds].astype(F32)
        for c in range(n_ca):
            a_ext[c, 0:hist_a - n_meta, :] = jnp.zeros((hist_a - n_meta, LANES), F32)
            a_ext[c, hist_a - n_meta:hist_a, :] = a_meta[:, c * LANES:(c + 1) * LANES]
        for c in range(n_cs):
            cv_ext[c, 0:hist_s, :] = cv_meta[n_meta - hist_s:, c * LANES:(c + 1) * LANES]

    @pl.when(step == 0)
    def _no_previous_blocks():
        act_a[1] = jnp.zeros(act_a.shape[1:], BF16)
        act_s[1] = jnp.zeros(act_s.shape[1:], BF16)
        m_buf[old] = jnp.zeros(m_buf.shape[1:], BF16)

    def stage1(r0):
        rws = slice(r0, r0 + rows)
        a = glu(pu_ref[rws, 0:dc], pu_ref[rws, o_ag:o_ag + dc])
        cv = pu_ref[rws, o_cg:o_cg + ds].astype(F32) * pu_ref[rws, o_v:o_v + ds].astype(F32)
        for c in range(n_ca):
            a_ext[c, hist_a + r0:hist_a + r0 + rows, :] = a[:, c * LANES:(c + 1) * LANES]
        for c in range(n_cs):
            cv_ext[c, hist_s + r0:hist_s + r0 + rows, :] = cv[:, c * LANES:(c + 1) * LANES]

        xs = []
        for c in range(n_ca):
            lanes = slice(c * LANES, (c + 1) * LANES)
            acc = jnp.broadcast_to(cb_ref[:, lanes], (rows, LANES))
            for k in range(k_conf):
                start = r0 + hist_a - (k_conf - 1) + k
                acc = acc + cw_ref[k:k + 1, lanes] * a_ext[c, start:start + rows, :]
            xs.append(acc)
        tot = xs[0]
        for x in xs[1:]:
            tot = tot + x
        mu = jnp.sum(tot, axis=-1, keepdims=True) * (1.0 / dc)
        dev = [x - mu for x in xs]
        sq = dev[0] * dev[0]
        for d in dev[1:]:
            sq = sq + d * d
        var = jnp.sum(sq, axis=-1, keepdims=True) * (1.0 / dc)
        inv = lax.rsqrt(var + LN_EPS)
        for c in range(n_ca):
            lanes = slice(c * LANES, (c + 1) * LANES)
            y = dev[c] * inv * lg_ref[:, lanes] + lb_ref[:, lanes]
            act_a[a_new, rws, lanes] =(y * _sigmoid(y)).astype(BF16)

        for c in range(n_cs):
            lanes = slice(c * LANES, (c + 1) * LANES)
            acc = None
            for k in range(k_short):
                start = r0 + hist_s - (k_short - 1) + k
                term = sw_ref[k:k + 1, lanes] * cv_ext[c, start:start + rows, :]
                acc = term if acc is None else acc + term
            bgate = pu_ref[rws, o_bg + c * LANES:o_bg + (c + 1) * LANES]
            act_s[a_new, rws, lanes] =(bgate.astype(F32) * acc).astype(BF16)

    def stage2(n0):
        cols = slice(n0, n0 + n_chunk)
        ga_ref, gb_ref = (ga0_ref, gb0_ref) if n0 < half else (ga1_ref, gb1_ref)
        gcols = slice(n0 % half, n0 % half + n_chunk)
        ya = jnp.dot(act_a[a_old], pw_ref[:, cols], preferred_element_type=F32)
        yb = jnp.dot(act_s[a_old], wsb_ref[:, cols], preferred_element_type=F32)
        ga = _sigmoid(ga_ref[:, gcols].astype(F32) + bg_ref[:, cols])
        gb = _sigmoid(gb_ref[:, gcols].astype(F32) + bg_ref[:, dm + n0:dm + n0 + n_chunk])
        m_buf[new, :, cols] = (ga * ya + gb * yb).astype(BF16)

    def stage3(n0):
        cols = slice(n0, n0 + n_chunk)
        mix_buf[:, cols] = jnp.dot(m_buf[old], wo_ref[:, cols],
                                   preferred_element_type=F32)

    for r in range(n_row_chunks):
        stage1(r * rows)
        if (r + 1) % rows_per_mxu_chunk == 0:
            n0 = (r // rows_per_mxu_chunk) * n_chunk
            stage3(n0)
            stage2(n0)

    for r in range(n_row_chunks):
        rws = slice(r * rows, (r + 1) * rows)
        mix = mix_buf[rws, :]
        inv = lax.rsqrt(jnp.mean(mix * mix, axis=-1, keepdims=True) + RMS_EPS)
        h1_ref[rws, :] = x_ref[rws, :] + mix * inv * gpost_ref[...]

    m_buf[old] = m_buf[new]

    for c in range(n_ca):
        a_ext[c, 0:hist_a, :] = a_ext[c, tt:tt + hist_a, :]
    for c in range(n_cs):
        cv_ext[c, 0:hist_s, :] = cv_ext[c, tt:tt + hist_s, :]


def _mixer(proj, x2d, meta_proj, b_gates, conf_dw_w, conf_dw_b, conf_ln_g, conf_ln_b,
           pw_bf16, short_dw_w, wsb_bf16, wo_bf16, g_post, *, batch, seq, tt):
    dm = x2d.shape[1]
    dc = conf_dw_w.shape[1]
    ds = short_dw_w.shape[1]
    k_conf = conf_dw_w.shape[0]
    k_short = short_dw_w.shape[0]
    n_meta = meta_proj.shape[0]
    n_cols = proj.shape[1]
    u_cols = 2 * dc + 3 * ds
    half = dm // 2
    assert n_cols == u_cols + 2 * dm and u_cols % half == 0
    g_blk = u_cols // half
    hist_a = 2 * SUBLANES_BF16
    hist_s = SUBLANES_F32
    assert hist_a >= k_conf - 1 and hist_a >= n_meta and n_meta >= hist_s >= k_short - 1
    n_t = seq // tt
    n_blocks = batch * n_t
    kern = functools.partial(
        _mixer_kernel, tt=tt, n_t=n_t, dc=dc, ds=ds, dm=dm, k_conf=k_conf,
        k_short=k_short, n_meta=n_meta, hist_a=hist_a, hist_s=hist_s,
        rows=2 * SUBLANES_BF16)
    stage_blk = lambda s, lag: jnp.clip(s - lag, 0, n_blocks - 1)
    const = lambda shape: pl.BlockSpec(shape, lambda s: (0, 0))
    gate =lambda k: pl.BlockSpec((tt, half), lambda s: (stage_blk(s, 1), g_blk + k))
    return pl.pallas_call(
        kern,
        grid=(n_blocks + 2,),
        in_specs=[
            pl.BlockSpec((tt, u_cols), lambda s: (stage_blk(s, 0), 0)),
            gate(0), gate(1), gate(2), gate(3),
            pl.BlockSpec((tt, dm), lambda s: (stage_blk(s, 2), 0)),
            const((n_meta, n_cols)),
            const((1, 2 * dm)),
            const((k_conf, dc)),
            const((1, dc)),
            const((1, dc)),
            const((1, dc)),
            const((dc, dm)),
            const((k_short, ds)),
            const((ds, dm)),
            const((dm, dm)),
            const((1, dm)),
        ],
        out_specs=pl.BlockSpec((tt, dm), lambda s: (stage_blk(s, 2), 0)),
        out_shape=jax.ShapeDtypeStruct((batch * seq, dm), F32),
        scratch_shapes=[
            pltpu.VMEM((dc // LANES, hist_a + tt, LANES), F32),
            pltpu.VMEM((ds // LANES, hist_s + tt, LANES), F32),
            pltpu.VMEM((2, tt, dc), BF16),
            pltpu.VMEM((2, tt, ds), BF16),
            pltpu.VMEM((2, tt, dm), BF16),
            pltpu.VMEM((tt, dm), F32),
        ],
        compiler_params=pltpu.CompilerParams(
            dimension_semantics=("arbitrary",),
            vmem_limit_bytes=VMEM_LIMIT_BYTES),
        name="mixer",
    )(proj, proj, proj, proj, proj, x2d, meta_proj, b_gates, conf_dw_w, conf_dw_b,
      conf_ln_g, conf_ln_b, pw_bf16, short_dw_w, wsb_bf16, wo_bf16, g_post)


def _mlp_kernel(h_ref, gpre_ref, wup_ref, wdn_ref, gpost_ref, o_ref, n_ref, *,
                n_chunk):
    j = pl.program_id(1)

    @pl.when(j == 0)
    def _normalise():
        h = h_ref[...]
        inv = lax.rsqrt(jnp.mean(h * h, axis=-1, keepdims=True) + RMS_EPS)
        n_ref[...] = (h * inv * gpre_ref[...]).astype(BF16)

        o_ref[...] = jnp.zeros_like(o_ref)

    u = jnp.dot(n_ref[...], wup_ref[...], preferred_element_type=F32)
    u = jnp.maximum(u, 0.0)
    act = (u * u).astype(BF16)
    for n0 in range(0, o_ref.shape[1], n_chunk):
        cols = slice(n0, n0 + n_chunk)
        o_ref[:, cols] += jnp.dot(act, wdn_ref[:, cols], preferred_element_type=F32)

    @pl.when(j == pl.num_programs(1) - 1)
    def _finish():
        f = o_ref[...]
        inv = lax.rsqrt(jnp.mean(f * f, axis=-1, keepdims=True) + RMS_EPS)
        o_ref[...] = h_ref[...] + f * inv * gpost_ref[...]


def _mlp(h1, g_pre, wup_bf16, wdn_bf16, g_post, *, tm, tf):
    rows, d = h1.shape
    f = wup_bf16.shape[1]
    return pl.pallas_call(
        functools.partial(_mlp_kernel, n_chunk=4 * LANES),
        grid=(rows // tm, f // tf),
        in_specs=[
            pl.BlockSpec((tm, d), lambda i, j: (i, 0)),
            pl.BlockSpec((1, d), lambda i, j: (0, 0)),
            pl.BlockSpec((d, tf), lambda i, j: (0, j)),
            pl.BlockSpec((tf, d), lambda i, j: (j, 0)),
            pl.BlockSpec((1, d), lambda i, j: (0, 0)),
        ],
        out_specs=pl.BlockSpec((tm, d), lambda i, j: (i, 0)),
        out_shape=jax.ShapeDtypeStruct((rows, d), F32),
        scratch_shapes=[pltpu.VMEM((tm, d), BF16)],
        compiler_params=pltpu.CompilerParams(
            dimension_semantics=("parallel", "arbitrary"),
            vmem_limit_bytes=VMEM_LIMIT_BYTES),
        name="mlp",
    )(h1, g_pre, wup_bf16, wdn_bf16, g_post)


def kernel(x, meta, g_pre_mix, w_in, b_gates, conf_dw_w, conf_dw_b, conf_ln_g,
           conf_ln_b, conf_w_pw, short_dw_w, short_w_out, w_o, g_post_mix,
           g_pre_mlp, w_up, w_down, g_post_mlp):
    batch, seq, dm = x.shape
    depth = w_in.shape[0]
    n_meta = meta.shape[0]
    assert depth == 1, "meta-token history seeding assumes a single layer"

    x2d = x.reshape(batch * seq, dm)
    l = 0
    row = lambda v: v[l].reshape(1, -1)
    w_in_b = w_in[l].astype(BF16)
    in_cols = w_in_b.shape[1]

    proj, (w_pw_b, w_sb_b, w_o_b, w_up_b, w_down_b) = _rms_in_proj(
        x2d, row(g_pre_mix), w_in_b,
        side_weights=(conf_w_pw[l], short_w_out[l], w_o[l], w_up[l], w_down[l]),
        tm=1024, tn=1536)
    meta_proj, _ = _rms_in_proj(meta.astype(x.dtype), row(g_pre_mix), w_in_b,
                                tm=n_meta, tn=in_cols // 3)
    h1 = _mixer(proj, x2d, meta_proj, row(b_gates), conf_dw_w[l], row(conf_dw_b),
                row(conf_ln_g), row(conf_ln_b), w_pw_b, short_dw_w[l], w_sb_b, w_o_b,
                row(g_post_mix), batch=batch, seq=seq, tt=256)
    out = _mlp(h1, row(g_pre_mlp), w_up_b, w_down_b, row(g_post_mlp), tm=1024, tf=512)
    return out.reshape(batch, seq, dm)
```

```python
import functools

import jax
import jax.numpy as jnp
from jax import lax
from jax.experimental import pallas as pl
from jax.experimental.pallas import tpu as pltpu

RMS_EPS = 1e-6
LN_EPS = 1e-5

LANES = 128
SUBLANES_F32 = 8
SUBLANES_BF16 = 16
VMEM_LIMIT_BYTES = 60 * 1024 * 1024

BF16 = jnp.bfloat16
F32 = jnp.float32


def _sigmoid(x):
    return 1.0 / (1.0 + jnp.exp(-x))


def _rms_in_proj_kernel(x_ref, g_ref, w_ref, *rest, n_side):
    side_in = rest[:n_side]
    o_ref = rest[n_side]
    side_out = rest[n_side + 1:2 * n_side + 1]
    n_ref = rest[2 * n_side + 1]

    @pl.when(pl.program_id(1) == 0)
    def _normalise():
        x = x_ref[...]
        inv = lax.rsqrt(jnp.mean(x * x, axis=-1, keepdims=True) + RMS_EPS)
        n_ref[...] = (x * inv * g_ref[...]).astype(BF16)

    n_chunk = 6 * LANES
    assert o_ref.shape[1] % n_chunk == 0
    for n0 in range(0, o_ref.shape[1], n_chunk):
        cols = slice(n0, n0 + n_chunk)
        o_ref[:, cols] = jnp.dot(n_ref[...], w_ref[:, cols],
                                 preferred_element_type=F32).astype(o_ref.dtype)
    for src_ref, dst_ref in zip(side_in, side_out):
        dst_ref[...] = src_ref[...].astype(dst_ref.dtype)


def _rms_in_proj(x2d, gain, w_bf16, side_weights=(), *, tm, tn):
    m, d = x2d.shape
    n = w_bf16.shape[1]
    n_j = n // tn
    n_steps = (m // tm) * n_j
    side_specs = []
    for w in side_weights:
        n_blk = n_steps
        while w.shape[0] % n_blk or (w.shape[0] // n_blk) % SUBLANES_BF16:
            n_blk -= 1
        side_specs.append(pl.BlockSpec(
            (w.shape[0] // n_blk, w.shape[1]),
            lambda i, j, n_blk=n_blk: (jnp.minimum(i * n_j + j, n_blk - 1), 0)))
    outs = pl.pallas_call(
        functools.partial(_rms_in_proj_kernel, n_side=len(side_weights)),
        grid=(m // tm, n_j),
        in_specs=[
            pl.BlockSpec((tm, d), lambda i, j: (i, 0)),
            pl.BlockSpec((1, d), lambda i, j: (0, 0)),
            pl.BlockSpec((d, tn), lambda i, j: (0, j)),
            *side_specs,
        ],
        out_specs=[pl.BlockSpec((tm, tn), lambda i, j: (i, j)), *side_specs],
        out_shape=[jax.ShapeDtypeStruct((m, n), BF16),
                   *(jax.ShapeDtypeStruct(w.shape, BF16) for w in side_weights)],
        scratch_shapes=[pltpu.VMEM((tm, d), BF16)],
        compiler_params=pltpu.CompilerParams(
            dimension_semantics=("arbitrary", "arbitrary"),
            vmem_limit_bytes=VMEM_LIMIT_BYTES),
        name="rms_in_proj",
    )(x2d, gain, w_bf16, *side_weights)
    return outs[0], list(outs[1:])


def _mixer_kernel(pu_ref, ga0_ref, ga1_ref, gb0_ref, gb1_ref, x_ref, meta_ref, bg_ref,
                  cw_ref, cb_ref, lg_ref, lb_ref, pw_ref, sw_ref, wsb_ref, wo_ref,
                  gpost_ref, h1_ref,
                  a_ext, cv_ext, act_a, act_s, m_buf, mix_buf, *,
                  tt, n_t, dc, ds, dm, k_conf, k_short, n_meta, hist_a, hist_s, rows):
    o_ag = dc
    o_bg = 2 * dc
    o_cg = o_bg + ds
    o_v = o_cg + ds
    n_ca = dc // LANES
    n_cs = ds // LANES
    n_row_chunks = tt // rows
    n_chunk = 2 * LANES
    n_mxu_chunks = dm // n_chunk
    assert n_row_chunks % n_mxu_chunks == 0
    rows_per_mxu_chunk = n_row_chunks // n_mxu_chunks
    half = dm // 2

    step = pl.program_id(0)
    a_new = lax.rem(step, 2)
    a_old = 1 - a_new
    new, old = 0, 1

    def glu(val, gate):
        return val.astype(F32) * _sigmoid(gate.astype(F32))

    @pl.when(lax.rem(step, n_t) == 0)
    def _seed_history():
        mp = meta_ref[...]
        a_meta = glu(mp[:, 0:dc], mp[:, o_ag:o_ag + dc])
        cv_meta = mp[:, o_cg:o_cg + ds].astype(F32) * mp[:, o_v:o_v + ds].astype(F32)
        for c in range(n_ca):
            a_ext[c, 0:hist_a - n_meta, :] = jnp.zeros((hist_a - n_meta, LANES), F32)
            a_ext[c, hist_a - n_meta:hist_a, :] = a_meta[:, c * LANES:(c + 1) * LANES]
        for c in range(n_cs):
            cv_ext[c, 0:hist_s, :] = cv_meta[n_meta - hist_s:, c * LANES:(c + 1) * LANES]

    @pl.when(step == 0)
    def _no_previous_blocks():
        act_a[1] = jnp.zeros(act_a.shape[1:], BF16)
        act_s[1] = jnp.zeros(act_s.shape[1:], BF16)
        m_buf[old] = jnp.zeros(m_buf.shape[1:], BF16)

    def stage1(r0):
        rws = slice(r0, r0 + rows)
        a = glu(pu_ref[rws, 0:dc], pu_ref[rws, o_ag:o_ag + dc])
        cv = pu_ref[rws, o_cg:o_cg + ds].astype(F32) * pu_ref[rws, o_v:o_v + ds].astype(F32)
        for c in range(n_ca):
            a_ext[c, hist_a + r0:hist_a + r0 + rows, :] = a[:, c * LANES:(c + 1) * LANES]
        for c in range(n_cs):
            cv_ext[c, hist_s + r0:hist_s + r0 + rows, :] = cv[:, c * LANES:(c + 1) * LANES]

        xs = []
        for c in range(n_ca):
            lanes = slice(c * LANES, (c + 1) * LANES)
            acc = jnp.broadcast_to(cb_ref[:, lanes], (rows, LANES))
            for k in range(k_conf):
                start = r0 + hist_a - (k_conf - 1) + k
                acc = acc + cw_ref[k:k + 1, lanes] * a_ext[c, start:start + rows, :]
            xs.append(acc)
        tot = xs[0]
        for x in xs[1:]:
            tot = tot + x
        mu = jnp.sum(tot, axis=-1, keepdims=True) * (1.0 / dc)
        dev = [x - mu for x in xs]
        sq = dev[0] * dev[0]
        for d in dev[1:]:
            sq = sq + d * d
        var = jnp.sum(sq, axis=-1, keepdims=True) * (1.0 / dc)
        inv = lax.rsqrt(var + LN_EPS)
        for c in range(n_ca):
            lanes = slice(c * LANES, (c + 1) * LANES)
            y = dev[c] * inv * lg_ref[:, lanes] + lb_ref[:, lanes]
            act_a[a_new, rws, lanes] =(y * _sigmoid(y)).astype(BF16)

        for c in range(n_cs):
            lanes = slice(c * LANES, (c + 1) * LANES)
            acc = None
            for k in range(k_short):
                start = r0 + hist_s - (k_short - 1) + k
                term = sw_ref[k:k + 1, lanes] * cv_ext[c, start:start + rows, :]
                acc = term if acc is None else acc + term
            bgate = pu_ref[rws, o_bg + c * LANES:o_bg + (c + 1) * LANES]
            act_s[a_new, rws, lanes] =(bgate.astype(F32) * acc).astype(BF16)

    def stage2(n0):
        cols = slice(n0, n0 + n_chunk)
        ga_ref, gb_ref = (ga0_ref, gb0_ref) if n0 < half else (ga1_ref, gb1_ref)
        gcols = slice(n0 % half, n0 % half + n_chunk)
        ya = jnp.dot(act_a[a_old], pw_ref[:, cols], preferred_element_type=F32)
        yb = jnp.dot(act_s[a_old], wsb_ref[:, cols], preferred_element_type=F32)
        ga = _sigmoid(ga_ref[:, gcols].astype(F32) + bg_ref[:, cols])
        gb = _sigmoid(gb_ref[:, gcols].astype(F32) + bg_ref[:, dm + n0:dm + n0 + n_chunk])
        m_buf[new, :, cols] = (ga * ya + gb * yb).astype(BF16)

    def stage3(n0):
        cols = slice(n0, n0 + n_chunk)
        mix_buf[:, cols] = jnp.dot(m_buf[old], wo_ref[:, cols],
                                   preferred_element_type=F32)

    for r in range(n_row_chunks):
        stage1(r * rows)
        if (r + 1) % rows_per_mxu_chunk == 0:
            n0 = (r // rows_per_mxu_chunk) * n_chunk
            stage3(n0)
            stage2(n0)

    for r in range(n_row_chunks):
        rws = slice(r * rows, (r + 1) * rows)
        mix = mix_buf[rws, :]
        inv = lax.rsqrt(jnp.mean(mix * mix, axis=-1, keepdims=True) + RMS_EPS)
        h1_ref[rws, :] = x_ref[rws, :] + mix * inv * gpost_ref[...]

    m_buf[old] = m_buf[new]

    for c in range(n_ca):
        a_ext[c, 0:hist_a, :] = a_ext[c, tt:tt + hist_a, :]
    for c in range(n_cs):
        cv_ext[c, 0:hist_s, :] = cv_ext[c, tt:tt + hist_s, :]


def _mixer(proj, x2d, meta_proj, b_gates, conf_dw_w, conf_dw_b, conf_ln_g, conf_ln_b,
           pw_bf16, short_dw_w, wsb_bf16, wo_bf16, g_post, *, batch, seq, tt):
    dm = x2d.shape[1]
    dc = conf_dw_w.shape[1]
    ds = short_dw_w.shape[1]
    k_conf = conf_dw_w.shape[0]
    k_short = short_dw_w.shape[0]
    n_meta = meta_proj.shape[0]
    n_cols = proj.shape[1]
    u_cols = 2 * dc + 3 * ds
    half = dm // 2
    assert n_cols == u_cols + 2 * dm and u_cols % half == 0
    g_blk = u_cols // half
    hist_a = 2 * SUBLANES_BF16
    hist_s = SUBLANES_F32
    assert hist_a >= k_conf - 1 and hist_a >= n_meta and n_meta >= hist_s >= k_short - 1
    n_t = seq // tt
    n_blocks = batch * n_t
    kern = functools.partial(
        _mixer_kernel, tt=tt, n_t=n_t, dc=dc, ds=ds, dm=dm, k_conf=k_conf,
        k_short=k_short, n_meta=n_meta, hist_a=hist_a, hist_s=hist_s,
        rows=2 * SUBLANES_BF16)
    stage_blk = lambda s, lag: jnp.clip(s - lag, 0, n_blocks - 1)
    const = lambda shape: pl.BlockSpec(shape, lambda s: (0, 0))
    gate = lambda k: pl.BlockSpec((tt, half), lambda s: (stage_blk(s, 1), g_blk + k))
    return pl.pallas_call(
        kern,
        grid=(n_blocks + 2,),
        in_specs=[
            pl.BlockSpec((tt, u_cols), lambda s: (stage_blk(s, 0), 0)),
            gate(0), gate(1), gate(2), gate(3),
            pl.BlockSpec((tt, dm), lambda s: (stage_blk(s, 2), 0)),
            const((n_meta, n_cols)),
            const((1, 2 * dm)),
            const((k_conf, dc)),
            const((1, dc)),
            const((1, dc)),
            const((1, dc)),
            const((dc, dm)),
            const((k_short, ds)),
            const((ds, dm)),
            const((dm, dm)),
            const((1, dm)),
        ],
        out_specs=pl.BlockSpec((tt, dm), lambda s: (stage_blk(s, 2), 0)),
        out_shape=jax.ShapeDtypeStruct((batch * seq, dm), F32),
        scratch_shapes=[
            pltpu.VMEM((dc // LANES, hist_a + tt, LANES), F32),
            pltpu.VMEM((ds // LANES, hist_s + tt, LANES), F32),
            pltpu.VMEM((2, tt, dc), BF16),
            pltpu.VMEM((2, tt, ds), BF16),
            pltpu.VMEM((2, tt, dm), BF16),
            pltpu.VMEM((tt, dm), F32),
        ],
        compiler_params=pltpu.CompilerParams(
            dimension_semantics=("arbitrary",),
            vmem_limit_bytes=VMEM_LIMIT_BYTES),
        name="mixer",
    )(proj, proj, proj, proj, proj, x2d, meta_proj, b_gates, conf_dw_w, conf_dw_b,
      conf_ln_g, conf_ln_b, pw_bf16, short_dw_w, wsb_bf16, wo_bf16, g_post)


def _mlp_kernel(h_ref, gpre_ref, wup_ref, wdn_ref, gpost_ref, o_ref, n_ref, *,
                n_chunk):
    j = pl.program_id(1)

    @pl.when(j == 0)
    def _normalise():
        h = h_ref[...]
        inv = lax.rsqrt(jnp.mean(h * h, axis=-1, keepdims=True) + RMS_EPS)
        n_ref[...] = (h * inv * gpre_ref[...]).astype(BF16)

        o_ref[...] = jnp.zeros_like(o_ref)

    u = jnp.dot(n_ref[...], wup_ref[...], preferred_element_type=F32)
    u = jnp.maximum(u, 0.0)
    act = (u * u).astype(BF16)
    for n0 in range(0, o_ref.shape[1], n_chunk):
        cols = slice(n0, n0 + n_chunk)
        o_ref[:, cols] += jnp.dot(act, wdn_ref[:, cols], preferred_element_type=F32)

    @pl.when(j == pl.num_programs(1) - 1)
    def _finish():
        f = o_ref[...]
        inv = lax.rsqrt(jnp.mean(f * f, axis=-1, keepdims=True) + RMS_EPS)
        o_ref[...] = h_ref[...] + f * inv * gpost_ref[...]


def _mlp(h1, g_pre, wup_bf16, wdn_bf16, g_post, *, tm, tf):
    rows, d = h1.shape
    f = wup_bf16.shape[1]
    return pl.pallas_call(
        functools.partial(_mlp_kernel, n_chunk=4 * LANES),
        grid=(rows // tm, f // tf),
        in_specs=[
            pl.BlockSpec((tm, d), lambda i, j: (i, 0)),
            pl.BlockSpec((1, d), lambda i, j: (0, 0)),
            pl.BlockSpec((d, tf), lambda i, j: (0, j)),
            pl.BlockSpec((tf, d), lambda i, j: (j, 0)),
            pl.BlockSpec((1, d), lambda i, j: (0, 0)),
        ],
        out_specs=pl.BlockSpec((tm, d), lambda i, j: (i, 0)),
        out_shape=jax.ShapeDtypeStruct((rows, d), F32),
        scratch_shapes=[pltpu.VMEM((tm, d), BF16)],
        compiler_params=pltpu.CompilerParams(
            dimension_semantics=("parallel", "arbitrary"),
            vmem_limit_bytes=VMEM_LIMIT_BYTES),
        name="mlp",
    )(h1, g_pre, wup_bf16, wdn_bf16, g_post)


def kernel(x, meta, g_pre_mix, w_in, b_gates, conf_dw_w, conf_dw_b, conf_ln_g,
           conf_ln_b, conf_w_pw, short_dw_w, short_w_out, w_o, g_post_mix,
           g_pre_mlp, w_up, w_down, g_post_mlp):
    batch, seq, dm = x.shape
    depth = w_in.shape[0]
    n_meta = meta.shape[0]
    assert depth == 1, "meta-token history seeding assumes a single layer"

    x2d = x.reshape(batch * seq, dm)
    l = 0
    row = lambda v: v[l].reshape(1, -1)
    w_in_b = w_in[l].astype(BF16)
    in_cols = w_in_b.shape[1]

    proj, (w_pw_b, w_sb_b, w_o_b, w_up_b, w_down_b) = _rms_in_proj(
        x2d, row(g_pre_mix), w_in_b,
        side_weights=(conf_w_pw[l], short_w_out[l], w_o[l], w_up[l], w_down[l]),
        tm=1024, tn=2304)
    meta_proj, _ = _rms_in_proj(meta.astype(x.dtype), row(g_pre_mix), w_in_b,
                                tm=n_meta, tn=in_cols // 3)
    h1 = _mixer(proj, x2d, meta_proj, row(b_gates), conf_dw_w[l], row(conf_dw_b),
                row(conf_ln_g), row(conf_ln_b), w_pw_b, short_dw_w[l], w_sb_b, w_o_b,
                row(g_post_mix), batch=batch, seq=seq, tt=256)
    out = _mlp(h1, row(g_pre_mlp), w_up_b, w_down_b, row(g_post_mlp), tm=1024, tf=512)
    return out.reshape(batch, seq, dm)
```

```python
import functools

import jax
import jax.numpy as jnp
from jax import lax
from jax.experimental import pallas as pl
from jax.experimental.pallas import tpu as pltpu

RMS_EPS = 1e-6
LN_EPS = 1e-5

LANES = 128
SUBLANES_F32 = 8
SUBLANES_BF16 = 16
VMEM_LIMIT_BYTES = 60 * 1024 * 1024

BF16 = jnp.bfloat16
F32 = jnp.float32


def _sigmoid(x):
    return 1.0 / (1.0 + jnp.exp(-x))


def _rms_in_proj_kernel(x_ref, g_ref, w_ref, gb_ref, *rest, n_side, gate_col0):
    side_in = rest[:n_side]
    o_ref = rest[n_side]
    side_out = rest[n_side + 1:2 * n_side + 1]
    n_ref = rest[2 * n_side + 1]

    @pl.when(pl.program_id(1) == 0)
    def _normalise():
        x = x_ref[...]
        inv = lax.rsqrt(jnp.mean(x * x, axis=-1, keepdims=True) + RMS_EPS)
        n_ref[...] = (x * inv * g_ref[...]).astype(BF16)

    n_chunk = 6 * LANES
    assert o_ref.shape[1] % n_chunk == 0
    tn = o_ref.shape[1]
    for n0 in range(0, tn, n_chunk):
        cols = slice(n0, n0 + n_chunk)
        y = jnp.dot(n_ref[...], w_ref[:, cols], preferred_element_type=F32)
        col = (pl.program_id(1) * tn + n0
               + lax.broadcasted_iota(jnp.int32, (1, n_chunk), 1))
        y = jnp.where(col >= gate_col0, _sigmoid(y + gb_ref[:, cols]), y)
        o_ref[:, cols] = y.astype(o_ref.dtype)
    for src_ref, dst_ref in zip(side_in, side_out):
        dst_ref[...] = src_ref[...].astype(dst_ref.dtype)


def _rms_in_proj(x2d, gain, w_bf16, gate_bias, side_weights=(), *, tm, tn):
    m, d = x2d.shape
    n = w_bf16.shape[1]
    gate_col0 = n - gate_bias.shape[1]
    bias_row = jnp.pad(gate_bias, ((0, 0), (gate_col0, 0)))
    n_j = n // tn
    n_steps = (m // tm) * n_j
    side_specs = []
    for w in side_weights:
        n_blk = n_steps
        while w.shape[0] % n_blk or (w.shape[0] // n_blk) % SUBLANES_BF16:
            n_blk -= 1
        side_specs.append(pl.BlockSpec(
            (w.shape[0] // n_blk, w.shape[1]),
            lambda i, j, n_blk=n_blk: (jnp.minimum(i * n_j + j, n_blk - 1), 0)))
    outs = pl.pallas_call(
        functools.partial(_rms_in_proj_kernel, n_side=len(side_weights),
                          gate_col0=gate_col0),
        grid=(m // tm, n_j),
        in_specs=[
            pl.BlockSpec((tm, d), lambda i, j: (i, 0)),
            pl.BlockSpec((1, d), lambda i, j: (0, 0)),
            pl.BlockSpec((d, tn), lambda i, j: (0, j)),
            pl.BlockSpec((1, tn), lambda i, j: (0, j)),
            *side_specs,
        ],
        out_specs=[pl.BlockSpec((tm, tn), lambda i, j: (i, j)), *side_specs],
        out_shape=[jax.ShapeDtypeStruct((m, n), BF16),
                   *(jax.ShapeDtypeStruct(w.shape, BF16) for w in side_weights)],
        scratch_shapes=[pltpu.VMEM((tm, d), BF16)],
        compiler_params=pltpu.CompilerParams(
            dimension_semantics=("arbitrary", "arbitrary"),
            vmem_limit_bytes=VMEM_LIMIT_BYTES),
        name="rms_in_proj",
    )(x2d, gain, w_bf16, bias_row, *side_weights)
    return outs[0], list(outs[1:])


def _mixer_kernel(pu_ref, ga0_ref, ga1_ref, gb0_ref, gb1_ref, x_ref, meta_ref, bg_ref,
                  cw_ref, cb_ref, lg_ref, lb_ref, pw_ref, sw_ref, wsb_ref, wo_ref,
                  gpost_ref, h1_ref,
                  a_ext, cv_ext, act_a, act_s, m_buf, mix_buf, *,
                  tt, n_t, dc, ds, dm, k_conf, k_short, n_meta, hist_a, hist_s, rows):
    o_ag = dc
    o_bg = 2 * dc
    o_cg = o_bg + ds
    o_v = o_cg + ds
    n_ca = dc // LANES
    n_cs = ds // LANES
    n_row_chunks = tt // rows
    n_chunk = 2 * LANES
    n_mxu_chunks = dm // n_chunk
    assert n_row_chunks % n_mxu_chunks == 0
    rows_per_mxu_chunk = n_row_chunks // n_mxu_chunks
    half = dm // 2

    step = pl.program_id(0)
    a_new = lax.rem(step, 2)
    a_old = 1 - a_new
    new, old = 0, 1

    def glu(val, gate):
        return val.astype(F32) * _sigmoid(gate.astype(F32))

    @pl.when(lax.rem(step, n_t) == 0)
    def _seed_history():
        mp = meta_ref[...]
        a_meta = glu(mp[:, 0:dc], mp[:, o_ag:o_ag + dc])
        cv_meta = mp[:, o_cg:o_cg + ds].astype(F32) * mp[:, o_v:o_v + ds].astype(F32)
        for c in range(n_ca):
            a_ext[c, 0:hist_a - n_meta, :] = jnp.zeros((hist_a - n_meta, LANES), F32)
            a_ext[c, hist_a - n_meta:hist_a, :] = a_meta[:, c * LANES:(c + 1) * LANES]
        for c in range(n_cs):
            cv_ext[c, 0:hist_s, :] = cv_meta[n_meta - hist_s:, c * LANES:(c + 1) * LANES]

    @pl.when(step == 0)
    def _no_previous_blocks():
        act_a[1] = jnp.zeros(act_a.shape[1:], BF16)
        act_s[1] = jnp.zeros(act_s.shape[1:], BF16)
        m_buf[old] = jnp.zeros(m_buf.shape[1:], BF16)

    def stage1(r0):
        rws = slice(r0, r0 + rows)
        a = glu(pu_ref[rws, 0:dc], pu_ref[rws, o_ag:o_ag + dc])
        cv = pu_ref[rws, o_cg:o_cg + ds].astype(F32) * pu_ref[rws, o_v:o_v + ds].astype(F32)
        for c in range(n_ca):
            a_ext[c, hist_a + r0:hist_a + r0 + rows, :] = a[:, c * LANES:(c + 1) * LANES]
        for c in range(n_cs):
            cv_ext[c, hist_s + r0:hist_s + r0 + rows, :] = cv[:, c * LANES:(c + 1) * LANES]

        xs = []
        for c in range(n_ca):
            lanes = slice(c * LANES, (c + 1) * LANES)
            acc = jnp.broadcast_to(cb_ref[:, lanes], (rows, LANES))
            for k in range(k_conf):
                start = r0 + hist_a - (k_conf - 1) + k
                acc = acc + cw_ref[k:k + 1, lanes] * a_ext[c, start:start + rows, :]
            xs.append(acc)
        tot = xs[0]
        for x in xs[1:]:
            tot = tot + x
        mu = jnp.sum(tot, axis=-1, keepdims=True) * (1.0 / dc)
        dev = [x - mu for x in xs]
        sq = dev[0] * dev[0]
        for d in dev[1:]:
            sq = sq + d * d
        var = jnp.sum(sq, axis=-1, keepdims=True) * (1.0 / dc)
        inv = lax.rsqrt(var + LN_EPS)
        for c in range(n_ca):
            lanes = slice(c * LANES, (c + 1) * LANES)
            y = dev[c] * inv * lg_ref[:, lanes] + lb_ref[:, lanes]
            act_a[a_new, rws, lanes] =(y * _sigmoid(y)).astype(BF16)

        for c in range(n_cs):
            lanes = slice(c * LANES, (c + 1) * LANES)
            acc = None
            for k in range(k_short):
                start = r0 + hist_s - (k_short - 1) + k
                term = sw_ref[k:k + 1, lanes] * cv_ext[c, start:start + rows, :]
                acc = term if acc is None else acc + term
            bgate = pu_ref[rws, o_bg + c * LANES:o_bg + (c + 1) * LANES]
            act_s[a_new, rws, lanes] =(bgate.astype(F32) * acc).astype(BF16)

    def stage2(n0):
        cols = slice(n0, n0 + n_chunk)
        ga_ref, gb_ref = (ga0_ref, gb0_ref) if n0 < half else (ga1_ref, gb1_ref)
        gcols = slice(n0 % half, n0 % half + n_chunk)
        ya = jnp.dot(act_a[a_old], pw_ref[:, cols], preferred_element_type=F32)
        yb = jnp.dot(act_s[a_old], wsb_ref[:, cols], preferred_element_type=F32)
        ga = ga_ref[:, gcols].astype(F32)
        gb = gb_ref[:, gcols].astype(F32)
        m_buf[new, :, cols] = (ga * ya + gb * yb).astype(BF16)

    def stage3(n0):
        cols = slice(n0, n0 + n_chunk)
        mix_buf[:, cols] = jnp.dot(m_buf[old], wo_ref[:, cols],
                                   preferred_element_type=F32)

    for r in range(n_row_chunks):
        stage1(r * rows)
        if (r + 1) % rows_per_mxu_chunk == 0:
            n0 = (r // rows_per_mxu_chunk) * n_chunk
            stage3(n0)
            stage2(n0)

    for r in range(n_row_chunks):
        rws = slice(r * rows, (r + 1) * rows)
        mix = mix_buf[rws, :]
        inv = lax.rsqrt(jnp.mean(mix * mix, axis=-1, keepdims=True) + RMS_EPS)
        h1_ref[rws, :] = x_ref[rws, :] + mix * inv * gpost_ref[...]

    m_buf[old] = m_buf[new]

    for c in range(n_ca):
        a_ext[c, 0:hist_a, :] = a_ext[c, tt:tt + hist_a, :]
    for c in range(n_cs):
        cv_ext[c, 0:hist_s, :] = cv_ext[c, tt:tt + hist_s, :]


def _mixer(proj, x2d, meta_proj, b_gates, conf_dw_w, conf_dw_b, conf_ln_g, conf_ln_b,
           pw_bf16, short_dw_w, wsb_bf16, wo_bf16, g_post, *, batch, seq, tt):
    dm = x2d.shape[1]
    dc = conf_dw_w.shape[1]
    ds = short_dw_w.shape[1]
    k_conf = conf_dw_w.shape[0]
    k_short = short_dw_w.shape[0]
    n_meta = meta_proj.shape[0]
    n_cols = proj.shape[1]
    u_cols = 2 * dc + 3 * ds
    half = dm // 2
    assert n_cols == u_cols + 2 * dm and u_cols % half == 0
    g_blk = u_cols // half
    hist_a = 2 * SUBLANES_BF16
    hist_s = SUBLANES_F32
    assert hist_a >= k_conf - 1 and hist_a >= n_meta and n_meta >= hist_s >= k_short - 1
    n_t = seq // tt
    n_blocks = batch * n_t
    kern = functools.partial(
        _mixer_kernel, tt=tt, n_t=n_t, dc=dc, ds=ds, dm=dm, k_conf=k_conf,
        k_short=k_short, n_meta=n_meta, hist_a=hist_a, hist_s=hist_s,
        rows=2 * SUBLANES_BF16)
    stage_blk = lambda s, lag: jnp.clip(s - lag, 0, n_blocks - 1)
    const = lambda shape: pl.BlockSpec(shape, lambda s: (0, 0))
    gate = lambda k: pl.BlockSpec((tt, half), lambda s: (stage_blk(s, 1), g_blk + k))
    return pl.pallas_call(
        kern,
        grid=(n_blocks + 2,),
        in_specs=[
            pl.BlockSpec((tt, u_cols), lambda s: (stage_blk(s, 0), 0)),
            gate(0), gate(1), gate(2), gate(3),
            pl.BlockSpec((tt, dm), lambda s: (stage_blk(s, 2), 0)),
            const((n_meta, n_cols)),
            const((1, 2 * dm)),
            const((k_conf, dc)),
            const((1, dc)),
            const((1, dc)),
            const((1, dc)),
            const((dc, dm)),
            const((k_short, ds)),
            const((ds, dm)),
            const((dm, dm)),
            const((1, dm)),
        ],
        out_specs=pl.BlockSpec((tt, dm), lambda s: (stage_blk(s, 2), 0)),
        out_shape=jax.ShapeDtypeStruct((batch * seq, dm), F32),
        scratch_shapes=[
            pltpu.VMEM((dc // LANES, hist_a + tt, LANES), F32),
            pltpu.VMEM((ds // LANES, hist_s + tt, LANES), F32),
            pltpu.VMEM((2, tt, dc), BF16),
            pltpu.VMEM((2, tt, ds), BF16),
            pltpu.VMEM((2, tt, dm), BF16),
            pltpu.VMEM((tt, dm), F32),
        ],
        compiler_params=pltpu.CompilerParams(
            dimension_semantics=("arbitrary",),
            vmem_limit_bytes=VMEM_LIMIT_BYTES),
        name="mixer",
    )(proj, proj, proj, proj, proj, x2d, meta_proj, b_gates, conf_dw_w, conf_dw_b,
      conf_ln_g, conf_ln_b, pw_bf16, short_dw_w, wsb_bf16, wo_bf16, g_post)


def _mlp_kernel(h_ref, gpre_ref, wup_ref, wdn_ref, gpost_ref, o_ref, n_ref, *,
                n_chunk):
    j = pl.program_id(1)

    @pl.when(j == 0)
    def _normalise():
        h = h_ref[...]
        inv = lax.rsqrt(jnp.mean(h * h, axis=-1, keepdims=True) + RMS_EPS)
        n_ref[...] = (h * inv * gpre_ref[...]).astype(BF16)

        o_ref[...] = jnp.zeros_like(o_ref)

    u = jnp.dot(n_ref[...], wup_ref[...], preferred_element_type=F32)
    u = jnp.maximum(u, 0.0)
    act = (u * u).astype(BF16)
    for n0 in range(0, o_ref.shape[1], n_chunk):
        cols = slice(n0, n0 + n_chunk)
        o_ref[:, cols] += jnp.dot(act, wdn_ref[:, cols], preferred_element_type=F32)

    @pl.when(j == pl.num_programs(1) - 1)
    def _finish():
        f = o_ref[...]
        inv = lax.rsqrt(jnp.mean(f * f, axis=-1, keepdims=True) + RMS_EPS)
        o_ref[...] = h_ref[...] + f * inv * gpost_ref[...]


def _mlp(h1, g_pre, wup_bf16, wdn_bf16, g_post, *, tm, tf):
    rows, d = h1.shape
    f = wup_bf16.shape[1]
    return pl.pallas_call(
        functools.partial(_mlp_kernel, n_chunk=4 * LANES),
        grid=(rows // tm, f // tf),
        in_specs=[
            pl.BlockSpec((tm, d), lambda i, j: (i, 0)),
            pl.BlockSpec((1, d), lambda i, j: (0, 0)),
            pl.BlockSpec((d, tf), lambda i, j: (0, j)),
            pl.BlockSpec((tf, d), lambda i, j: (j, 0)),
            pl.BlockSpec((1, d), lambda i, j: (0, 0)),
        ],
        out_specs=pl.BlockSpec((tm, d), lambda i, j: (i, 0)),
        out_shape=jax.ShapeDtypeStruct((rows, d), F32),
        scratch_shapes=[pltpu.VMEM((tm, d), BF16)],
        compiler_params=pltpu.CompilerParams(
            dimension_semantics=("parallel", "arbitrary"),
            vmem_limit_bytes=VMEM_LIMIT_BYTES),
        name="mlp",
    )(h1, g_pre, wup_bf16, wdn_bf16, g_post)


def kernel(x, meta, g_pre_mix, w_in, b_gates, conf_dw_w, conf_dw_b, conf_ln_g,
           conf_ln_b, conf_w_pw, short_dw_w, short_w_out, w_o, g_post_mix,
           g_pre_mlp, w_up, w_down, g_post_mlp):
    batch, seq, dm = x.shape
    depth = w_in.shape[0]
    n_meta = meta.shape[0]
    assert depth == 1, "meta-token history seeding assumes a single layer"

    x2d = x.reshape(batch * seq, dm)
    l = 0
    row = lambda v: v[l].reshape(1, -1)
    w_in_b = w_in[l].astype(BF16)
    in_cols = w_in_b.shape[1]

    proj, (w_pw_b, w_sb_b, w_o_b, w_up_b, w_down_b) = _rms_in_proj(
        x2d, row(g_pre_mix), w_in_b, row(b_gates),
        side_weights=(conf_w_pw[l], short_w_out[l], w_o[l], w_up[l], w_down[l]),
        tm=1024, tn=2304)
    meta_proj, _ = _rms_in_proj(meta.astype(x.dtype), row(g_pre_mix), w_in_b,
                                row(b_gates), tm=n_meta, tn=in_cols // 3)
    h1 = _mixer(proj, x2d, meta_proj, row(b_gates), conf_dw_w[l], row(conf_dw_b),
                row(conf_ln_g), row(conf_ln_b), w_pw_b, short_dw_w[l], w_sb_b, w_o_b,
                row(g_post_mix), batch=batch, seq=seq, tt=256)
    out = _mlp(h1, row(g_pre_mlp), w_up_b, w_down_b, row(g_post_mlp), tm=1024, tf=512)
    return out.reshape(batch, seq, dm)
```
